```python
import math
import jax, jax.numpy as jnp
from jax import lax
import numpy as np

D_MODEL = 2048
BATCH = 2
SEQ = 4096
DEPTH = 1

PLE_DIM = 256
D_FF = 5632
EPS = 1e-6
MLA_HEADS = 8
QK_NOPE = 128
QK_ROPE = 64
QK_HEAD = QK_NOPE + QK_ROPE
V_HEAD = 128
Q_LORA = 512
KV_LORA = 256
ROPE_BASE = 10000.0
Q_BLOCK = 128
MLA_OUT = MLA_HEADS * V_HEAD
MLA_IN = Q_LORA + KV_LORA + QK_ROPE
RWKV_HEAD = 64
RWKV_HEADS = 16
RWKV_DIM = RWKV_HEADS * RWKV_HEAD
DECAY_LORA = 64
AAA_LORA = 64
GATE_LORA = 128
GN_EPS = 64e-5
RWKV_IN = 3 * RWKV_DIM + DECAY_LORA + AAA_LORA + GATE_LORA
D_MIX = MLA_OUT + RWKV_DIM
D_IN = MLA_IN + RWKV_IN

kernel_name = "hymba_mla_rwkv7_macaron_ple"


def rmsnorm(x, g, eps=EPS):
    xf = x.astype(jnp.float32)
    y = xf * lax.rsqrt(jnp.mean(xf * xf, axis=-1, keepdims=True) + eps)
    return (y * g.astype(jnp.float32)).astype(x.dtype)


def swiglu(h, w_gate, w_up, w_down):
    return (jax.nn.silu(h @ w_gate) * (h @ w_up)) @ w_down


def rope_tables(seq_len, dtype):
    inv_freq = 1.0 / (ROPE_BASE ** (jnp.arange(0, QK_ROPE, 2, dtype=jnp.float32) / QK_ROPE))
    ang = jnp.arange(seq_len, dtype=jnp.float32)[:, None] * inv_freq[None, :]
    return jnp.cos(ang)[:, None, :].astype(dtype), jnp.sin(ang)[:, None, :].astype(dtype)


def apply_rope(x, cos, sin):
    x1, x2 = jnp.split(x, 2, axis=-1)
    return jnp.concatenate([x1 * cos - x2 * sin, x2 * cos + x1 * sin], axis=-1)


def mla_group(z_mla, q_a_norm, w_q_b, kv_a_norm, w_kv_b, q_norm, k_norm):
    B, S, _ = z_mla.shape
    q_lat = z_mla[..., :Q_LORA]
    kv_lat = z_mla[..., Q_LORA:Q_LORA + KV_LORA]
    k_pe = z_mla[..., Q_LORA + KV_LORA:]
    q = (rmsnorm(q_lat, q_a_norm) @ w_q_b).reshape(B, S, MLA_HEADS, QK_HEAD)
    kv = (rmsnorm(kv_lat, kv_a_norm) @ w_kv_b).reshape(B, S, MLA_HEADS, QK_NOPE + V_HEAD)
    k_nope, v = kv[..., :QK_NOPE], kv[..., QK_NOPE:]
    k_pe = jnp.broadcast_to(k_pe[:, :, None, :], (B, S, MLA_HEADS, QK_ROPE))
    k = jnp.concatenate([k_nope, k_pe], axis=-1)
    q = rmsnorm(q, q_norm)
    k = rmsnorm(k, k_norm)
    cos, sin = rope_tables(S, q.dtype)
    q = jnp.concatenate([q[..., :QK_NOPE], apply_rope(q[..., QK_NOPE:], cos, sin)], axis=-1)
    k = jnp.concatenate([k[..., :QK_NOPE], apply_rope(k[..., QK_NOPE:], cos, sin)], axis=-1)
    q = q * (1.0 / math.sqrt(QK_HEAD))

    n_blk = S // Q_BLOCK
    qb = q.reshape(B, n_blk, Q_BLOCK, MLA_HEADS, QK_HEAD).transpose(1, 0, 2, 3, 4)
    k_pos = jnp.arange(S)

    def attend(args):
        q_i, blk = args
        s = jnp.einsum('bqhd,bkhd->bhqk', q_i, k).astype(jnp.float32)
        q_pos = blk * Q_BLOCK + jnp.arange(Q_BLOCK)
        causal = k_pos[None, :] <= q_pos[:, None]
        s = jnp.where(causal[None, None], s, -1e30)
        pr = jax.nn.softmax(s, axis=-1).astype(v.dtype)
        return jnp.einsum('bhqk,bkhd->bqhd', pr, v)

    o = lax.map(attend, (qb, jnp.arange(n_blk)))
    return o.transpose(1, 0, 2, 3, 4).reshape(B, S, MLA_OUT)


def rwkv7_group(z_rwkv, mu_shift, w0, w_w2, a0, w_a2, w_g2, k_k, k_a, r_k, ln_x_w, ln_x_b):
    B, S, _ = z_rwkv.shape
    H, N = RWKV_HEADS, RWKV_HEAD
    prev = jnp.pad(z_rwkv, ((0, 0), (1, 0), (0, 0)))[:, :-1]
    zs = z_rwkv + (prev - z_rwkv) * mu_shift
    o1, o2, o3 = RWKV_DIM, 2 * RWKV_DIM, 3 * RWKV_DIM
    o4, o5 = o3 + DECAY_LORA, o3 + DECAY_LORA + AAA_LORA
    r, k, v = zs[..., :o1], zs[..., o1:o2], zs[..., o2:o3]
    w_lo, a_lo, g_lo = zs[..., o3:o4], zs[..., o4:o5], zs[..., o5:]
    w = -jax.nn.softplus(-(w0 + jnp.tanh(w_lo) @ w_w2)) - 0.5
    decay = jnp.exp(-jnp.exp(w.astype(jnp.float32)))
    a = jax.nn.sigmoid(a0 + a_lo @ w_a2)
    g = jax.nn.sigmoid(g_lo) @ w_g2
    kk = (k * k_k).reshape(B, S, H, N).astype(jnp.float32)
    kk = kk / jnp.maximum(jnp.sqrt(jnp.sum(kk * kk, axis=-1, keepdims=True)), 1e-12)
    k = k * (1.0 + (a - 1.0) * k_a)

    def heads(t):
        return t.reshape(B, S, H, N).astype(jnp.float32)

    r_h, k_h, v_h, a_h, w_h = heads(r), heads(k), heads(v), heads(a), heads(decay)

    def step(state, inp):
        r_t, w_t, k_t, v_t, kk_t, a_t = inp
        sa = jnp.einsum('bhvk,bhk->bhv', state, -kk_t)
        state = (state * w_t[:, :, None, :]
                 + sa[..., None] * (kk_t * a_t)[:, :, None, :]
                 + v_t[..., None] * k_t[:, :, None, :])
        return state, jnp.einsum('bhvk,bhk->bhv', state, r_t)

    seq_first = lambda t: jnp.swapaxes(t, 0, 1)
    init = jnp.zeros((B, H, N, N), jnp.float32)
    _, o = lax.scan(step, init, tuple(seq_first(t) for t in (r_h, w_h, k_h, v_h, kk, a_h)))
    o = seq_first(o)
    mu = jnp.mean(o, axis=-1, keepdims=True)
    var = jnp.mean(jnp.square(o - mu), axis=-1, keepdims=True)
    o = (o - mu) * lax.rsqrt(var + GN_EPS)
    o = o * ln_x_w.reshape(H, N).astype(jnp.float32) + ln_x_b.reshape(H, N).astype(jnp.float32)
    bonus = jnp.sum(r_h * k_h * r_k.astype(jnp.float32), axis=-1, keepdims=True) * v_h
    o = (o + bonus).reshape(B, S, RWKV_DIM).astype(z_rwkv.dtype)
    return o * g


def setup_inputs(seed: int = 0) -> dict:
    key = jax.random.key(seed)
    ks = jax.random.split(key, 40)
    f32 = jnp.float32

    def dense(k, shape, fan_in, scale=1.0):
        return jax.random.normal(k, shape, f32) * (scale * fan_in ** -0.5)

    def gain(k, n):
        return 1.0 + 0.02 * jax.random.normal(k, (DEPTH, n), f32)

    L = DEPTH
    return {
        "x": jax.random.normal(ks[0], (BATCH, SEQ, D_MODEL), f32),
        "p": jax.random.normal(ks[1], (DEPTH, BATCH, SEQ, PLE_DIM), f32),
        "norm_ffn1": gain(ks[2], D_MODEL),
        "w1_gate": dense(ks[3], (L, D_MODEL, D_FF), D_MODEL),
        "w1_up": dense(ks[4], (L, D_MODEL, D_FF), D_MODEL),
        "w1_down": dense(ks[5], (L, D_FF, D_MODEL), D_FF),
        "norm_mix": gain(ks[6], D_MODEL),
        "w_in": dense(ks[7], (L, D_MODEL, D_IN), D_MODEL),
        "q_a_norm": gain(ks[8], Q_LORA),
        "w_q_b": dense(ks[9], (L, Q_LORA, MLA_HEADS * QK_HEAD), Q_LORA),
        "kv_a_norm": gain(ks[10], KV_LORA),
        "w_kv_b": dense(ks[11], (L, KV_LORA, MLA_HEADS * (QK_NOPE + V_HEAD)), KV_LORA),
        "q_norm": gain(ks[12], QK_HEAD),
        "k_norm": gain(ks[13], QK_HEAD),
        "mu_shift": jax.random.uniform(ks[14], (L, RWKV_IN), f32),
        "w0": jax.random.normal(ks[15], (L, RWKV_DIM), f32),
        "w_w2": dense(ks[16], (L, DECAY_LORA, RWKV_DIM), DECAY_LORA, 0.5),
        "a0": 0.1 * jax.random.normal(ks[17], (L, RWKV_DIM), f32),
        "w_a2": dense(ks[18], (L, AAA_LORA, RWKV_DIM), AAA_LORA),
        "w_g2": dense(ks[19], (L, GATE_LORA, RWKV_DIM), GATE_LORA),
        "k_k": 0.85 + 0.05 * jax.random.normal(ks[20], (L, RWKV_DIM), f32),
        "k_a": 1.0 + 0.05 * jax.random.normal(ks[21], (L, RWKV_DIM), f32),
        "r_k": 0.1 * jax.random.normal(ks[22], (L, RWKV_HEADS, RWKV_HEAD), f32),
        "ln_x_w": gain(ks[23], RWKV_DIM),
        "ln_x_b": 0.02 * jax.random.normal(ks[24], (L, RWKV_DIM), f32),
        "w_out": dense(ks[25], (L, D_MIX, D_MODEL), D_MIX),
        "norm_ffn2": gain(ks[26], D_MODEL),
        "w2_gate": dense(ks[27], (L, D_MODEL, D_FF), D_MODEL),
        "w2_up": dense(ks[28], (L, D_MODEL, D_FF), D_MODEL),
        "w2_down": dense(ks[29], (L, D_FF, D_MODEL), D_FF),
        "norm_ple": gain(ks[30], D_MODEL),
        "w_ple_gate": dense(ks[31], (L, D_MODEL, D_MODEL), D_MODEL),
        "w_ple_proj": dense(ks[32], (L, PLE_DIM, D_MODEL), PLE_DIM),
    }


def reference(x, p, norm_ffn1, w1_gate, w1_up, w1_down, norm_mix, w_in, q_a_norm, w_q_b,
              kv_a_norm, w_kv_b, q_norm, k_norm, mu_shift, w0, w_w2, a0, w_a2, w_g2, k_k, k_a,
              r_k, ln_x_w, ln_x_b, w_out, norm_ffn2, w2_gate, w2_up, w2_down, norm_ple,
              w_ple_gate, w_ple_proj):
    for i in range(DEPTH):
        x = x + 0.5 * swiglu(rmsnorm(x, norm_ffn1[i]), w1_gate[i], w1_up[i], w1_down[i])
        h = rmsnorm(x, norm_mix[i])
        z = h @ w_in[i]
        o_mla = mla_group(z[..., :MLA_IN], q_a_norm[i], w_q_b[i], kv_a_norm[i], w_kv_b[i],
                          q_norm[i], k_norm[i])
        o_rwkv = rwkv7_group(z[..., MLA_IN:], mu_shift[i], w0[i], w_w2[i], a0[i], w_a2[i],
                             w_g2[i], k_k[i], k_a[i], r_k[i], ln_x_w[i], ln_x_b[i])
        x = x + jnp.concatenate([o_mla, o_rwkv], axis=-1) @ w_out[i]
        x = x + 0.5 * swiglu(rmsnorm(x, norm_ffn2[i]), w2_gate[i], w2_up[i], w2_down[i])
        gate = jax.nn.sigmoid(rmsnorm(x, norm_ple[i]) @ w_ple_gate[i])
        x = x + gate * (p[i] @ w_ple_proj[i])
    return x
```

```python
import functools
import math

import jax
import jax.numpy as jnp
from jax import lax
from jax.experimental import pallas as pl
from jax.experimental.pallas import tpu as pltpu

F32 = jnp.float32
BF16 = jnp.bfloat16

EPS = 1e-6
GN_EPS = 64e-5
MLA_HEADS = 8
QK_NOPE = 128
QK_ROPE = 64
QK_HEAD = QK_NOPE + QK_ROPE
V_HEAD = 128
Q_LORA = 512
KV_LORA = 256
ROPE_BASE = 10000.0
RWKV_HEAD = 64
RWKV_HEADS = 16
RWKV_DIM = RWKV_HEADS * RWKV_HEAD
LANES = 128
CHUNK = 64
VMEM_LIMIT = 56 * 1024 * 1024


def _cparams(sem):
    return pltpu.CompilerParams(dimension_semantics=sem, vmem_limit_bytes=VMEM_LIMIT)


def _dot(a, b):
    return jnp.dot(a, b, preferred_element_type=F32)


def _rms(x, g):
    return x * lax.rsqrt(jnp.mean(x * x, axis=-1, keepdims=True) + EPS) * g


def _ffn_kernel(x_ref, g_ref, wg_ref, wu_ref, wd_ref, o_ref, h_ref, acc_ref):
    j = pl.program_id(1)

    @pl.when(j == 0)
    def _():
        h_ref[...] = _rms(x_ref[...], g_ref[...]).astype(BF16)
        acc_ref[...] = jnp.zeros_like(acc_ref)

    h = h_ref[...]
    gate = _dot(h, wg_ref[...])
    up = _dot(h, wu_ref[...])
    act = (gate * jax.nn.sigmoid(gate) * up).astype(BF16)
    acc_ref[...] += _dot(act, wd_ref[...])

    @pl.when(j == pl.num_programs(1) - 1)
    def _():
        o_ref[...] = x_ref[...] + 0.5 * acc_ref[...]


def _ffn(x, g, wg, wu, wd, tm, tf):
    n, d = x.shape
    dff = wg.shape[1]
    return pl.pallas_call(
        _ffn_kernel,
        grid=(n // tm, dff // tf),
        in_specs=[
            pl.BlockSpec((tm, d), lambda i, j: (i, 0)),
            pl.BlockSpec((1, d), lambda i, j: (0, 0)),
            pl.BlockSpec((d, tf), lambda i, j: (0, j)),
            pl.BlockSpec((d, tf), lambda i, j: (0, j)),
            pl.BlockSpec((tf, d), lambda i, j: (j, 0)),
        ],
        out_specs=pl.BlockSpec((tm, d), lambda i, j: (i, 0)),
        out_shape=jax.ShapeDtypeStruct((n, d), F32),
        scratch_shapes=[pltpu.VMEM((tm, d), BF16), pltpu.VMEM((tm, d), F32)],
        compiler_params=_cparams(("parallel", "arbitrary")),
        name="ffn",
    )(x, g, wg, wu, wd)


def _norm_mm_kernel(x_ref, g_ref, w_ref, o_ref, h_ref):
    @pl.when(pl.program_id(1) == 0)
    def _():
        h_ref[...] = _rms(x_ref[...], g_ref[...]).astype(BF16)

    o_ref[...] = _dot(h_ref[...], w_ref[...])


def _norm_mm(x, g, w, tm, tn):
    n, d = x.shape
    nc = w.shape[1]
    return pl.pallas_call(
        _norm_mm_kernel,
        grid=(n // tm, nc // tn),
        in_specs=[
            pl.BlockSpec((tm, d), lambda i, j: (i, 0)),
            pl.BlockSpec((1, d), lambda i, j: (0, 0)),
            pl.BlockSpec((d, tn), lambda i, j: (0, j)),
        ],
        out_specs=pl.BlockSpec((tm, tn), lambda i, j: (i, j)),
        out_shape=jax.ShapeDtypeStruct((n, nc), F32),
        scratch_shapes=[pltpu.VMEM((tm, d), BF16)],
        compiler_params=_cparams(("parallel", "arbitrary")),
        name="norm_mm",
    )(x, g, w)


def _seg2(x):
    lo = lax.broadcasted_iota(jnp.int32, x.shape, 1) < RWKV_HEAD
    s0 = jnp.sum(jnp.where(lo, x, 0.0), axis=-1, keepdims=True)
    s1 = jnp.sum(jnp.where(lo, 0.0, x), axis=-1, keepdims=True)
    return s0, s1


def _rope_pair(y, cos_t, sin_t):
    lane = lax.broadcasted_iota(jnp.int32, y.shape, 1)
    first = (lane % QK_ROPE) < (QK_ROPE // 2)
    rot = jnp.where(first, pltpu.roll(y, LANES - QK_ROPE // 2, axis=1), pltpu.roll(y, QK_ROPE // 2, axis=1))
    return y * cos_t + rot * sin_t


def _mla_prep_kernel(x_ref, gmix_ref, win_ref, qag_ref, wqb_ref, kvag_ref, wkvb_ref,
                     gqn_ref, gqp_ref, gkn_ref, gkp_ref, cos_ref, sin_ref,
                     q_ref, k_ref, v_ref):
    scale = 1.0 / math.sqrt(QK_HEAD)
    h = _rms(x_ref[...], gmix_ref[...]).astype(BF16)
    z = _dot(h, win_ref[...])
    q_lat = z[:, :Q_LORA]
    kv_lat = z[:, Q_LORA:Q_LORA + KV_LORA]
    kpe2 = z[:, Q_LORA + KV_LORA:]
    qf = _dot(_rms(q_lat, qag_ref[...]).astype(BF16), wqb_ref[...])
    kvf = _dot(_rms(kv_lat, kvag_ref[...]).astype(BF16), wkvb_ref[...])
    cos_t = cos_ref[...]
    sin_t = sin_ref[...]
    lo = lax.broadcasted_iota(jnp.int32, cos_t.shape, 1) < QK_ROPE
    kpe_ss, _ = _seg2(kpe2 * kpe2)
    nope_w = MLA_HEADS * QK_NOPE
    for j in range(MLA_HEADS // 2):
        qp = qf[:, nope_w + j * LANES: nope_w + (j + 1) * LANES]
        qs0, qs1 = _seg2(qp * qp)
        rs_q = []
        rs_k = []
        for e, qs in ((0, qs0), (1, qs1)):
            hd = 2 * j + e
            qn = qf[:, hd * QK_NOPE:(hd + 1) * QK_NOPE]
            rq = lax.rsqrt((jnp.sum(qn * qn, axis=-1, keepdims=True) + qs) * (1.0 / QK_HEAD) + EPS)
            q_ref[0, hd, :, :QK_NOPE] = (qn * rq * gqn_ref[...] * scale).astype(BF16)
            rs_q.append(rq)
            kn = kvf[:, hd * 2 * QK_NOPE: hd * 2 * QK_NOPE + QK_NOPE]
            rk = lax.rsqrt((jnp.sum(kn * kn, axis=-1, keepdims=True) + kpe_ss) * (1.0 / QK_HEAD) + EPS)
            k_ref[0, hd, :, :QK_NOPE] = (kn * rk * gkn_ref[...]).astype(BF16)
            rs_k.append(rk)
            v_ref[0, hd] = kvf[:, hd * 2 * QK_NOPE + QK_NOPE:(hd + 1) * 2 * QK_NOPE].astype(BF16)
        yq = _rope_pair(qp * jnp.where(lo, rs_q[0], rs_q[1]) * gqp_ref[...], cos_t, sin_t) * scale
        yk = _rope_pair(kpe2 * jnp.where(lo, rs_k[0], rs_k[1]) * gkp_ref[...], cos_t, sin_t)
        q_ref[0, 2 * j, :, QK_NOPE:] = jnp.where(lo, yq, 0.0).astype(BF16)
        q_ref[0, 2 * j + 1, :, QK_NOPE:] = jnp.where(lo, 0.0, yq).astype(BF16)
        k_ref[0, 2 * j, :, QK_NOPE:] = jnp.where(lo, yk, 0.0).astype(BF16)
        k_ref[0, 2 * j + 1, :, QK_NOPE:] = jnp.where(lo, 0.0, yk).astype(BF16)


def _mla_prep(x, gmix, win, qag, wqb, kvag, wkvb, gqn, gqp, gkn, gkp, cos_t, sin_t, batch, seq, tm):
    n, d = x.shape
    spt = seq // tm
    full = lambda a: pl.BlockSpec(a.shape, lambda i: (0,) * a.ndim)
    qk_w = 2 * LANES
    return pl.pallas_call(
        _mla_prep_kernel,
        grid=(n // tm,),
        in_specs=[
            pl.BlockSpec((tm, d), lambda i: (i, 0)),
            full(gmix), full(win), full(qag), full(wqb), full(kvag), full(wkvb),
            full(gqn), full(gqp), full(gkn), full(gkp),
            pl.BlockSpec((tm, LANES), lambda i: (i % spt, 0)),
            pl.BlockSpec((tm, LANES), lambda i: (i % spt, 0)),
        ],
        out_specs=[
            pl.BlockSpec((1, MLA_HEADS, tm, qk_w), lambda i: (i // spt, 0, i % spt, 0)),
            pl.BlockSpec((1, MLA_HEADS, tm, qk_w), lambda i: (i // spt, 0, i % spt, 0)),
            pl.BlockSpec((1, MLA_HEADS, tm, V_HEAD), lambda i: (i // spt, 0, i % spt, 0)),
        ],
        out_shape=[
            jax.ShapeDtypeStruct((batch, MLA_HEADS, seq, qk_w), BF16),
            jax.ShapeDtypeStruct((batch, MLA_HEADS, seq, qk_w), BF16),
            jax.ShapeDtypeStruct((batch, MLA_HEADS, seq, V_HEAD), BF16),
        ],
        compiler_params=_cparams(("parallel",)),
        name="mla_prep",
    )(x, gmix, win, qag, wqb, kvag, wkvb, gqn, gqp, gkn, gkp, cos_t, sin_t)


def _flash_kernel(q_ref, k_ref, v_ref, o_ref, m_ref, l_ref, acc_ref, *, tq):
    qi = pl.program_id(2)
    ki = pl.program_id(3)

    @pl.when(ki == 0)
    def _():
        m_ref[...] = jnp.full_like(m_ref, -1e30)
        l_ref[...] = jnp.zeros_like(l_ref)
        acc_ref[...] = jnp.zeros_like(acc_ref)

    @pl.when(ki <= qi)
    def _():
        s = lax.dot_general(q_ref[0, 0], k_ref[0, 0], (((1,), (1,)), ((), ())), preferred_element_type=F32)
        row = qi * tq + lax.broadcasted_iota(jnp.int32, s.shape, 0)
        col = ki * tq + lax.broadcasted_iota(jnp.int32, s.shape, 1)
        s = jnp.where(col <= row, s, -1e30)
        m_prev = m_ref[...]
        m_new = jnp.maximum(m_prev, jnp.max(s, axis=-1, keepdims=True))
        alpha = jnp.exp(m_prev - m_new)
        p = jnp.exp(s - m_new)
        l_ref[...] = alpha * l_ref[...] + jnp.sum(p, axis=-1, keepdims=True)
        acc_ref[...] = alpha * acc_ref[...] + _dot(p.astype(BF16), v_ref[0, 0])
        m_ref[...] = m_new

    @pl.when(ki == pl.num_programs(3) - 1)
    def _():
        o_ref[0] = (acc_ref[...] / l_ref[...]).astype(o_ref.dtype)


def _flash(q, k, v, tq):
    b, hh, s, dk = q.shape
    dv = v.shape[-1]
    nq = s // tq
    return pl.pallas_call(
        functools.partial(_flash_kernel, tq=tq),
        grid=(b, hh, nq, nq),
        in_specs=[
            pl.BlockSpec((1, 1, tq, dk), lambda bi, h, qi, ki: (bi, h, qi, 0)),
            pl.BlockSpec((1, 1, tq, dk), lambda bi, h, qi, ki: (bi, h, jnp.minimum(ki, qi), 0)),
            pl.BlockSpec((1, 1, tq, dv), lambda bi, h, qi, ki: (bi, h, jnp.minimum(ki, qi), 0)),
        ],
        out_specs=pl.BlockSpec((1, tq, dv), lambda bi, h, qi, ki: (bi, qi, h)),
        out_shape=jax.ShapeDtypeStruct((b, s, hh * dv), BF16),
        scratch_shapes=[pltpu.VMEM((tq, 1), F32), pltpu.VMEM((tq, 1), F32), pltpu.VMEM((tq, dv), F32)],
        compiler_params=_cparams(("parallel", "parallel", "parallel", "arbitrary")),
        name="flash",
    )(q, k, v)


def _rwkv_prep_kernel(z_ref, zp_ref, mu_ref, w0_ref, ww_ref, a0_ref, wa_ref, wg_ref, kk_ref, ka_ref,
                      r_o, k_o, v_o, kk_o, a_o, lw_o, g_o, *, seq):
    tm = z_ref.shape[0]
    z = z_ref[...]
    first = (pl.program_id(0) * tm) % seq == 0
    prow = jnp.where(first, 0.0, zp_ref[7:8, :])
    rowid = lax.broadcasted_iota(jnp.int32, z.shape, 0)
    prev = jnp.where(rowid == 0, prow, pltpu.roll(z, 1, axis=0))
    zs = z + (prev - z) * mu_ref[...]
    d = RWKV_DIM
    r = zs[:, :d]
    k = zs[:, d:2 * d]
    v = zs[:, 2 * d:3 * d]
    wa_lo = zs[:, 3 * d:3 * d + LANES]
    g_lo = zs[:, 3 * d + LANES:]
    wpre = w0_ref[...] + _dot(jnp.tanh(wa_lo).astype(BF16), ww_ref[...])
    y = -wpre
    w = -(jnp.maximum(y, 0.0) + jnp.log(1.0 + jnp.exp(-jnp.abs(y)))) - 0.5
    a = jax.nn.sigmoid(a0_ref[...] + _dot(wa_lo.astype(BF16), wa_ref[...]))
    g = _dot(jax.nn.sigmoid(g_lo).astype(BF16), wg_ref[...])
    r_o[...] = r
    v_o[...] = v
    a_o[...] = a
    g_o[...] = g
    lw_o[...] = -jnp.exp(w)
    k_o[...] = k * (1.0 + (a - 1.0) * ka_ref[...])
    kk = k * kk_ref[...]
    lo = lax.broadcasted_iota(jnp.int32, (tm, LANES), 1) < RWKV_HEAD
    for t in range(d // LANES):
        kt = kk[:, t * LANES:(t + 1) * LANES]
        s0, s1 = _seg2(kt * kt)
        nrm = jnp.where(lo, jnp.maximum(jnp.sqrt(s0), 1e-12), jnp.maximum(jnp.sqrt(s1), 1e-12))
        kk_o[:, t * LANES:(t + 1) * LANES] = kt / nrm


def _rwkv_prep(z, mu, w0, ww, a0, wa, wg, kkp, kap, seq, tm):
    n, zin = z.shape
    d = RWKV_DIM
    full = lambda a: pl.BlockSpec(a.shape, lambda i: (0,) * a.ndim)
    out = pl.BlockSpec((tm, d), lambda i: (i, 0))
    return pl.pallas_call(
        functools.partial(_rwkv_prep_kernel, seq=seq),
        grid=(n // tm,),
        in_specs=[
            pl.BlockSpec((tm, zin), lambda i: (i, 0)),
            pl.BlockSpec((8, zin), lambda i: (jnp.maximum(i * (tm // 8) - 1, 0), 0)),
            full(mu), full(w0), full(ww), full(a0), full(wa), full(wg), full(kkp), full(kap),
        ],
        out_specs=[out] * 7,
        out_shape=[jax.ShapeDtypeStruct((n, d), F32)] * 7,
        compiler_params=_cparams(("parallel",)),
        name="rwkv_prep",
    )(z, z, mu, w0, ww, a0, wa, wg, kkp, kap)


def _split(x):
    hi = x.astype(BF16)
    return hi, (x - hi.astype(F32)).astype(BF16)


def _bmm(a, b, spec, passes):
    mm = lambda u, w: jnp.einsum(spec, u, w, preferred_element_type=F32)
    if passes == 1:
        return mm(a.astype(BF16), b.astype(BF16))
    ah, al = _split(a)
    bh, bl = _split(b)
    return mm(ah, bh) + mm(al, bh) + mm(ah, bl)


def _tiles(x):
    return jnp.stack([x[:, t * LANES:(t + 1) * LANES] for t in range(x.shape[1] // LANES)])


def _rwkv_scan_kernel(r_ref, k_ref, v_ref, kk_ref, a_ref, lw_ref, g_ref, rk_ref, lnw_ref, lnb_ref,
                      o_ref, h_ref, *, passes):
    c = CHUNK
    c2 = 2 * c

    @pl.when(pl.program_id(1) == 0)
    def _():
        h_ref[...] = jnp.zeros_like(h_ref)

    lw = lw_ref[...]
    ti = lax.broadcasted_iota(jnp.int32, (c, c), 0)
    si = lax.broadcasted_iota(jnp.int32, (c, c), 1)
    tril = jnp.where(si <= ti, 1.0, 0.0)
    lwh, lwl = _split(lw)
    lwl2 = (lw - lwh.astype(F32) - lwl.astype(F32)).astype(BF16)
    trilb = tril.astype(BF16)
    cum = _dot(trilb, lwh) + _dot(trilb, lwl) + _dot(trilb, lwl2)
    p = jnp.exp(cum)
    pinv = jnp.exp(-cum)
    pprev = jnp.exp(cum - lw)
    pend = jnp.exp(cum[c - 1:c, :] - cum)
    r = r_ref[...]
    k = k_ref[...]
    v = v_ref[...]
    kk = kk_ref[...]
    b = kk * a_ref[...]
    at = _tiles(-kk * pprev)
    bt = _tiles(b * pinv)
    kt = _tiles(k * pinv)
    rt = _tiles(r * p)
    bh = _tiles(b * pend)
    kh = _tiles(k * pend)
    vt = _tiles(v)
    pc = _tiles(jnp.exp(cum[c - 1:c, :]))

    lo = lax.broadcasted_iota(jnp.int32, (1, 1, LANES), 2) < RWKV_HEAD

    def bd(x):
        return jnp.concatenate([jnp.where(lo, x, 0.0), jnp.where(lo, 0.0, x)], axis=1)

    a_bd, b_bd, k_bd, r_bd, v_bd = bd(at), bd(bt), bd(kt), bd(rt), bd(vt)
    bh_bd, kh_bd = bd(bh), bd(kh)

    amat = _bmm(jnp.concatenate([a_bd, r_bd], axis=1), jnp.concatenate([b_bd, k_bd], axis=1),
                "pil,pjl->pij", passes)
    ri = lax.broadcasted_iota(jnp.int32, (1, c2, c2), 1)
    ci = lax.broadcasted_iota(jnp.int32, (1, c2, c2), 2)
    same = (ri // c) == (ci // c)
    strict = same & ((ri % c) > (ci % c))
    incl = same & ((ri % c) >= (ci % c))
    a_ab = jnp.where(strict, amat[:, :c2, :c2], 0.0)
    a_ak = jnp.where(strict, amat[:, :c2, c2:], 0.0)
    a_rb = jnp.where(incl, amat[:, c2:, :c2], 0.0)
    a_rk = jnp.where(incl, amat[:, c2:, c2:], 0.0)

    eye = jnp.where(ri == ci, 1.0, 0.0)
    tinv = eye + a_ab
    apow = a_ab
    for _ in range(int(math.log2(c)) - 1):
        apow = _bmm(apow, apow, "pij,pjk->pik", passes)
        tinv = tinv + _bmm(tinv, apow, "pij,pjk->pik", passes)

    akv = _bmm(a_ak, v_bd, "pij,pjk->pik", passes)
    wu = _bmm(tinv, jnp.concatenate([a_bd, akv], axis=2), "pij,pjk->pik", passes)
    rhs = jnp.concatenate([wu, jnp.concatenate([jnp.zeros_like(v_bd), v_bd], axis=2)], axis=1)
    top = _bmm(jnp.concatenate([a_rb, a_rk], axis=2), rhs, "pij,pjk->pik", passes)
    bot = _bmm(jnp.concatenate([bh_bd, kh_bd], axis=1), rhs, "psk,psn->pkn", passes)
    q_eff = r_bd + top[:, :, :c2]
    o_intra = top[:, :, c2:]
    m_mat = eye * pc + bot[:, :, :c2]
    g_mat = bot[:, :, c2:]

    h0 = h_ref[...]
    seq_out = _bmm(jnp.concatenate([q_eff, m_mat], axis=1), h0, "pij,pjk->pik", passes)
    o_bd = seq_out[:, :c2, :] + o_intra
    h_ref[...] = seq_out[:, c2:, :] + g_mat
    o = jnp.where(lo, o_bd[:, :c, :], o_bd[:, c:, :])

    rk = rk_ref[...]
    lnw = lnw_ref[...]
    lnb = lnb_ref[...]
    g = g_ref[...]
    lo2 = lax.broadcasted_iota(jnp.int32, (c, LANES), 1) < RWKV_HEAD
    inv_n = 1.0 / RWKV_HEAD
    for t in range(RWKV_DIM // LANES):
        sl = slice(t * LANES, (t + 1) * LANES)
        ot = o[t]
        s0, s1 = _seg2(ot)
        cen = ot - jnp.where(lo2, s0, s1) * inv_n
        q0, q1 = _seg2(cen * cen)
        on = cen * lax.rsqrt(jnp.where(lo2, q0, q1) * inv_n + GN_EPS) * lnw[:, sl] + lnb[:, sl]
        b0, b1 = _seg2(r[:, sl] * k[:, sl] * rk[:, sl])
        on = on + jnp.where(lo2, b0, b1) * v[:, sl]
        o_ref[:, sl] = (on * g[:, sl]).astype(o_ref.dtype)


def _rwkv_scan(r, k, v, kk, a, lw, g, rk, lnw, lnb, batch, seq, passes):
    n, d = r.shape
    nck = seq // CHUNK
    blk = pl.BlockSpec((CHUNK, d), lambda bi, ci: (bi * nck + ci, 0))
    full = lambda x: pl.BlockSpec(x.shape, lambda bi, ci: (0,) * x.ndim)
    return pl.pallas_call(
        functools.partial(_rwkv_scan_kernel, passes=passes),
        grid=(batch, nck),
        in_specs=[blk] * 7 + [full(rk), full(lnw), full(lnb)],
        out_specs=blk,
        out_shape=jax.ShapeDtypeStruct((n, d), BF16),
        scratch_shapes=[pltpu.VMEM((d // LANES, LANES, LANES), F32)],
        compiler_params=_cparams(("parallel", "arbitrary")),
        name="rwkv_scan",
    )(r, k, v, kk, a, lw, g, rk, lnw, lnb)


def _out_proj_kernel(x_ref, oa_ref, ob_ref, wa_ref, wb_ref, o_ref):
    o_ref[...] = x_ref[...] + _dot(oa_ref[...], wa_ref[...]) + _dot(ob_ref[...], wb_ref[...])


def _out_proj(x, oa, ob, wa, wb, tm, tn):
    n, d = x.shape
    ka = oa.shape[1]
    kb = ob.shape[1]
    return pl.pallas_call(
        _out_proj_kernel,
        grid=(n // tm, d // tn),
        in_specs=[
            pl.BlockSpec((tm, tn), lambda i, j: (i, j)),
            pl.BlockSpec((tm, ka), lambda i, j: (i, 0)),
            pl.BlockSpec((tm, kb), lambda i, j: (i, 0)),
            pl.BlockSpec((ka, tn), lambda i, j: (0, j)),
            pl.BlockSpec((kb, tn), lambda i, j: (0, j)),
        ],
        out_specs=pl.BlockSpec((tm, tn), lambda i, j: (i, j)),
        out_shape=jax.ShapeDtypeStruct((n, d), F32),
        compiler_params=_cparams(("parallel", "parallel")),
        name="out_proj",
    )(x, oa, ob, wa, wb)


def _ple_kernel(x_ref, g_ref, p_ref, wg_ref, wp_ref, o_ref, h_ref, *, tn):
    j = pl.program_id(1)

    @pl.when(j == 0)
    def _():
        h_ref[...] = _rms(x_ref[...], g_ref[...]).astype(BF16)

    gate = jax.nn.sigmoid(_dot(h_ref[...], wg_ref[...]))
    emb = _dot(p_ref[...].astype(BF16), wp_ref[...])
    col = pl.multiple_of(j * tn, LANES)
    o_ref[...] = x_ref[:, pl.ds(col, tn)] + gate * emb


def _ple(x, g, p, wg, wp, tm, tn):
    n, d = x.shape
    pd = p.shape[1]
    return pl.pallas_call(
        functools.partial(_ple_kernel, tn=tn),
        grid=(n // tm, d // tn),
        in_specs=[
            pl.BlockSpec((tm, d), lambda i, j: (i, 0)),
            pl.BlockSpec((1, d), lambda i, j: (0, 0)),
            pl.BlockSpec((tm, pd), lambda i, j: (i, 0)),
            pl.BlockSpec((d, tn), lambda i, j: (0, j)),
            pl.BlockSpec((pd, tn), lambda i, j: (0, j)),
        ],
        out_specs=pl.BlockSpec((tm, tn), lambda i, j: (i, j)),
        out_shape=jax.ShapeDtypeStruct((n, d), F32),
        scratch_shapes=[pltpu.VMEM((tm, d), BF16)],
        compiler_params=_cparams(("parallel", "arbitrary")),
        name="ple",
    )(x, g, p, wg, wp)


def _tile(n, pref):
    t = min(n, pref)
    assert n % t == 0, (n, t)
    return t


def _layer(x, p, norm_ffn1, w1_gate, w1_up, w1_down, norm_mix, w_in, q_a_norm, w_q_b, kv_a_norm, w_kv_b,
           q_norm, k_norm, mu_shift, w0, w_w2, a0, w_a2, w_g2, k_k, k_a, r_k, ln_x_w, ln_x_b, w_out,
           norm_ffn2, w2_gate, w2_up, w2_down, norm_ple, w_ple_gate, w_ple_proj):
    batch, seq, d = x.shape
    n = batch * seq
    row = lambda a: a.reshape(1, -1)
    bf = lambda a: a.astype(BF16)
    mla_in = Q_LORA + KV_LORA + QK_ROPE
    dff = w1_gate.shape[1]
    tf = _tile(dff, 512)
    tm_ffn = _tile(n, 512)

    xf = x.reshape(n, d)
    x1 = _ffn(xf, row(norm_ffn1), bf(w1_gate), bf(w1_up), bf(w1_down), tm_ffn, tf)

    w_mla = bf(jnp.concatenate([w_in[:, :mla_in], w_in[:, mla_in - QK_ROPE:mla_in]], axis=1))
    wq = w_q_b.reshape(Q_LORA, MLA_HEADS, QK_HEAD)
    wqb = bf(jnp.concatenate([wq[:, :, :QK_NOPE].reshape(Q_LORA, -1), wq[:, :, QK_NOPE:].reshape(Q_LORA, -1)], axis=1))
    inv_freq = 1.0 / (ROPE_BASE ** (jnp.arange(0, QK_ROPE, 2, dtype=F32) / QK_ROPE))
    ang = jnp.arange(seq, dtype=F32)[:, None] * inv_freq[None, :]
    cos, sin = jnp.cos(ang), jnp.sin(ang)
    cos_t = jnp.concatenate([cos, cos, cos, cos], axis=1)
    sin_t = jnp.concatenate([-sin, sin, -sin, sin], axis=1)
    pair = lambda a: row(jnp.concatenate([a, a]))
    tm_prep = _tile(seq, 256)
    q, k, v = _mla_prep(x1, row(norm_mix), w_mla, row(q_a_norm), wqb, row(kv_a_norm), bf(w_kv_b),
                        row(q_norm[:QK_NOPE]), pair(q_norm[QK_NOPE:]), row(k_norm[:QK_NOPE]),
                        pair(k_norm[QK_NOPE:]), cos_t, sin_t, batch, seq, tm_prep)
    o_mla = _flash(q, k, v, _tile(seq, 512)).reshape(n, MLA_HEADS * V_HEAD)

    z = _norm_mm(x1, row(norm_mix), bf(w_in[:, mla_in:]), _tile(n, 512), 13 * LANES)
    zero = jnp.zeros_like(w_w2)
    ww = bf(jnp.concatenate([w_w2, zero], axis=0))
    wa = bf(jnp.concatenate([zero, w_a2], axis=0))
    rw = _rwkv_prep(z, row(mu_shift), row(w0), ww, row(a0), wa, bf(w_g2), row(k_k), row(k_a), seq, tm_prep)
    o_rwkv = _rwkv_scan(*rw, row(r_k), row(ln_x_w), row(ln_x_b), batch, seq, 3)

    half = MLA_HEADS * V_HEAD
    x2 = _out_proj(x1, o_mla, o_rwkv, bf(w_out[:half]), bf(w_out[half:]), _tile(n, 512), _tile(d, 1024))
    x3 = _ffn(x2, row(norm_ffn2), bf(w2_gate), bf(w2_up), bf(w2_down), tm_ffn, tf)
    out = _ple(x3, row(norm_ple), p.reshape(n, -1), bf(w_ple_gate), bf(w_ple_proj), _tile(n, 512), _tile(d, 1024))
    return out.reshape(batch, seq, d)


def kernel(x, p, norm_ffn1, w1_gate, w1_up, w1_down, norm_mix, w_in, q_a_norm, w_q_b, kv_a_norm, w_kv_b, q_norm, k_norm, mu_shift, w0, w_w2, a0, w_a2, w_g2, k_k, k_a, r_k, ln_x_w, ln_x_b, w_out, norm_ffn2, w2_gate, w2_up, w2_down, norm_ple, w_ple_gate, w_ple_proj):
    depth = p.shape[0]
    for i in range(depth):
        x = _layer(x, p[i], norm_ffn1[i], w1_gate[i], w1_up[i], w1_down[i], norm_mix[i], w_in[i], q_a_norm[i],
                   w_q_b[i], kv_a_norm[i], w_kv_b[i], q_norm[i], k_norm[i], mu_shift[i], w0[i], w_w2[i], a0[i],
                   w_a2[i], w_g2[i], k_k[i], k_a[i], r_k[i], ln_x_w[i], ln_x_b[i], w_out[i], norm_ffn2[i],
                   w2_gate[i], w2_up[i], w2_down[i], norm_ple[i], w_ple_gate[i], w_ple_proj[i])
    return x
```

```python
import functools
import math

import jax
import jax.numpy as jnp
from jax import lax
from jax.experimental import pallas as pl
from jax.experimental.pallas import tpu as pltpu

F32 = jnp.float32
BF16 = jnp.bfloat16

EPS = 1e-6
GN_EPS = 64e-5
MLA_HEADS = 8
QK_NOPE = 128
QK_ROPE = 64
QK_HEAD = QK_NOPE + QK_ROPE
V_HEAD = 128
Q_LORA = 512
KV_LORA = 256
ROPE_BASE = 10000.0
RWKV_HEAD = 64
RWKV_HEADS = 16
RWKV_DIM = RWKV_HEADS * RWKV_HEAD
LANES = 128
CHUNK = 64
VMEM_LIMIT = 56 * 1024 * 1024


def _cparams(sem):
    return pltpu.CompilerParams(dimension_semantics=sem, vmem_limit_bytes=VMEM_LIMIT)


def _dot(a, b):
    return jnp.dot(a, b, preferred_element_type=F32)


def _rms(x, g):
    return x * lax.rsqrt(jnp.mean(x * x, axis=-1, keepdims=True) + EPS) * g


def _ffn_kernel(x_ref, g_ref, wg_ref, wu_ref, wd_ref, o_ref, h_ref, acc_ref):
    j = pl.program_id(1)

    @pl.when(j == 0)
    def _():
        h_ref[...] = _rms(x_ref[...], g_ref[...]).astype(BF16)
        acc_ref[...] = jnp.zeros_like(acc_ref)

    h = h_ref[...]
    gate = _dot(h, wg_ref[...])
    up = _dot(h, wu_ref[...])
    act = (gate * jax.nn.sigmoid(gate) * up).astype(BF16)
    acc_ref[...] += _dot(act, wd_ref[...])

    @pl.when(j == pl.num_programs(1) - 1)
    def _():
        o_ref[...] = x_ref[...] + 0.5 * acc_ref[...]


def _ffn(x, g, wg, wu, wd, tm, tf):
    n, d = x.shape
    dff = wg.shape[1]
    return pl.pallas_call(
        _ffn_kernel,
        grid=(n // tm, dff // tf),
        in_specs=[
            pl.BlockSpec((tm, d), lambda i, j: (i, 0)),
            pl.BlockSpec((1, d), lambda i, j: (0, 0)),
            pl.BlockSpec((d, tf), lambda i, j: (0, j)),
            pl.BlockSpec((d, tf), lambda i, j: (0, j)),
            pl.BlockSpec((tf, d), lambda i, j: (j, 0)),
        ],
        out_specs=pl.BlockSpec((tm, d), lambda i, j: (i, 0)),
        out_shape=jax.ShapeDtypeStruct((n, d), F32),
        scratch_shapes=[pltpu.VMEM((tm, d), BF16), pltpu.VMEM((tm, d), F32)],
        compiler_params=_cparams(("parallel", "arbitrary")),
        name="ffn",
    )(x, g, wg, wu, wd)


def _norm_mm_kernel(x_ref, g_ref, w_ref, o_ref, h_ref):
    @pl.when(pl.program_id(1) == 0)
    def _():
        h_ref[...] = _rms(x_ref[...], g_ref[...]).astype(BF16)

    o_ref[...] = _dot(h_ref[...], w_ref[...])


def _norm_mm(x, g, w, tm, tn):
    n, d = x.shape
    nc = w.shape[1]
    return pl.pallas_call(
        _norm_mm_kernel,
        grid=(n // tm, nc // tn),
        in_specs=[
            pl.BlockSpec((tm, d), lambda i, j: (i, 0)),
            pl.BlockSpec((1, d), lambda i, j: (0, 0)),
            pl.BlockSpec((d, tn), lambda i, j: (0, j)),
        ],
        out_specs=pl.BlockSpec((tm, tn), lambda i, j: (i, j)),
        out_shape=jax.ShapeDtypeStruct((n, nc), F32),
        scratch_shapes=[pltpu.VMEM((tm, d), BF16)],
        compiler_params=_cparams(("parallel", "arbitrary")),
        name="norm_mm",
    )(x, g, w)


def _seg2(x):
    lo = lax.broadcasted_iota(jnp.int32, x.shape, 1) < RWKV_HEAD
    s0 = jnp.sum(jnp.where(lo, x, 0.0), axis=-1, keepdims=True)
    s1 = jnp.sum(jnp.where(lo, 0.0, x), axis=-1, keepdims=True)
    return s0, s1


def _rope_pair(y, cos_t, sin_t):
    lane = lax.broadcasted_iota(jnp.int32, y.shape, 1)
    first = (lane % QK_ROPE) < (QK_ROPE // 2)
    rot = jnp.where(first, pltpu.roll(y, LANES - QK_ROPE // 2, axis=1), pltpu.roll(y, QK_ROPE // 2, axis=1))
    return y * cos_t + rot * sin_t


def _mla_prep_kernel(x_ref, gmix_ref, win_ref, qag_ref, wqb_ref, kvag_ref, wkvb_ref,
                     gqn_ref, gqp_ref, gkn_ref, gkp_ref, cos_ref, sin_ref,
                     q_ref, k_ref, v_ref):
    scale = 1.0 / math.sqrt(QK_HEAD)
    h = _rms(x_ref[...], gmix_ref[...]).astype(BF16)
    z = _dot(h, win_ref[...])
    q_lat = z[:, :Q_LORA]
    kv_lat = z[:, Q_LORA:Q_LORA + KV_LORA]
    kpe2 = z[:, Q_LORA + KV_LORA:]
    qf = _dot(_rms(q_lat, qag_ref[...]).astype(BF16), wqb_ref[...])
    kvf = _dot(_rms(kv_lat, kvag_ref[...]).astype(BF16), wkvb_ref[...])
    cos_t = cos_ref[...]
    sin_t = sin_ref[...]
    lo = lax.broadcasted_iota(jnp.int32, cos_t.shape, 1) < QK_ROPE
    kpe_ss, _ = _seg2(kpe2 * kpe2)
    nope_w = MLA_HEADS * QK_NOPE
    for j in range(MLA_HEADS // 2):
        qp = qf[:, nope_w + j * LANES: nope_w + (j + 1) * LANES]
        qs0, qs1 = _seg2(qp * qp)
        rs_q = []
        rs_k = []
        for e, qs in ((0, qs0), (1, qs1)):
            hd = 2 * j + e
            qn = qf[:, hd * QK_NOPE:(hd + 1) * QK_NOPE]
            rq = lax.rsqrt((jnp.sum(qn * qn, axis=-1, keepdims=True) + qs) * (1.0 / QK_HEAD) + EPS)
            q_ref[0, hd, :, :QK_NOPE] = (qn * rq * gqn_ref[...] * scale).astype(BF16)
            rs_q.append(rq)
            kn = kvf[:, hd * 2 * QK_NOPE: hd * 2 * QK_NOPE + QK_NOPE]
            rk = lax.rsqrt((jnp.sum(kn * kn, axis=-1, keepdims=True) + kpe_ss) * (1.0 / QK_HEAD) + EPS)
            k_ref[0, hd, :, :QK_NOPE] = (kn * rk * gkn_ref[...]).astype(BF16)
            rs_k.append(rk)
            v_ref[0, hd] = kvf[:, hd * 2 * QK_NOPE + QK_NOPE:(hd + 1) * 2 * QK_NOPE].astype(BF16)
        yq = _rope_pair(qp * jnp.where(lo, rs_q[0], rs_q[1]) * gqp_ref[...], cos_t, sin_t) * scale
        yk = _rope_pair(kpe2 * jnp.where(lo, rs_k[0], rs_k[1]) * gkp_ref[...], cos_t, sin_t)
        q_ref[0, 2 * j, :, QK_NOPE:] = jnp.where(lo, yq, 0.0).astype(BF16)
        q_ref[0, 2 * j + 1, :, QK_NOPE:] = jnp.where(lo, 0.0, yq).astype(BF16)
        k_ref[0, 2 * j, :, QK_NOPE:] = jnp.where(lo, yk, 0.0).astype(BF16)
        k_ref[0, 2 * j + 1, :, QK_NOPE:] = jnp.where(lo, 0.0, yk).astype(BF16)


def _mla_prep(x, gmix, win, qag, wqb, kvag, wkvb, gqn, gqp, gkn, gkp, cos_t, sin_t, batch, seq, tm):
    n, d = x.shape
    spt = seq // tm
    full = lambda a: pl.BlockSpec(a.shape, lambda i: (0,) * a.ndim)
    qk_w = 2 * LANES
    return pl.pallas_call(
        _mla_prep_kernel,
        grid=(n // tm,),
        in_specs=[
            pl.BlockSpec((tm, d), lambda i: (i, 0)),
            full(gmix), full(win), full(qag), full(wqb), full(kvag), full(wkvb),
            full(gqn), full(gqp), full(gkn), full(gkp),
            pl.BlockSpec((tm, LANES), lambda i: (i % spt, 0)),
            pl.BlockSpec((tm, LANES), lambda i: (i % spt, 0)),
        ],
        out_specs=[
            pl.BlockSpec((1, MLA_HEADS, tm, qk_w), lambda i: (i // spt, 0, i % spt, 0)),
            pl.BlockSpec((1, MLA_HEADS, tm, qk_w), lambda i: (i // spt, 0, i % spt, 0)),
            pl.BlockSpec((1, MLA_HEADS, tm, V_HEAD), lambda i: (i // spt, 0, i % spt, 0)),
        ],
        out_shape=[
            jax.ShapeDtypeStruct((batch, MLA_HEADS, seq, qk_w), BF16),
            jax.ShapeDtypeStruct((batch, MLA_HEADS, seq, qk_w), BF16),
            jax.ShapeDtypeStruct((batch, MLA_HEADS, seq, V_HEAD), BF16),
        ],
        compiler_params=_cparams(("parallel",)),
        name="mla_prep",
    )(x, gmix, win, qag, wqb, kvag, wkvb, gqn, gqp, gkn, gkp, cos_t, sin_t)


def _flash_kernel(q_ref, k_ref, v_ref, o_ref, m_ref, l_ref, acc_ref, *, tk, nsub):
    qi = pl.program_id(2)
    m_ref[...] = jnp.full_like(m_ref, -1e30)
    l_ref[...] = jnp.zeros_like(l_ref)
    acc_ref[...] = jnp.zeros_like(acc_ref)
    rep = tk // LANES
    causal = lax.broadcasted_iota(jnp.int32, (tk, tk), 1) <= lax.broadcasted_iota(jnp.int32, (tk, tk), 0)

    def step(sub, j, diag):
        rows = pl.ds(sub * tk, tk)
        kv = pl.ds(pl.multiple_of(j * tk, tk), tk)
        s = lax.dot_general(q_ref[0, 0, rows, :], k_ref[0, 0, kv, :], (((1,), (1,)), ((), ())),
                            preferred_element_type=F32)
        if diag:
            s = jnp.where(causal, s, -1e30)
        m_prev = m_ref[rows, :]
        m_new = jnp.maximum(m_prev, jnp.max(s, axis=-1, keepdims=True))
        alpha = jnp.exp(m_prev - m_new)
        p = jnp.exp(s - pltpu.repeat(m_new, rep, axis=1))
        l_ref[rows, :] = alpha * l_ref[rows, :] + jnp.sum(p, axis=-1, keepdims=True)
        acc_ref[rows, :] = alpha * acc_ref[rows, :] + _dot(p.astype(BF16), v_ref[0, 0, kv, :])
        m_ref[rows, :] = m_new

    def body(j, carry):
        for sub in range(nsub):
            step(sub, j, False)
        return carry

    lax.fori_loop(0, qi * nsub, body, 0)
    for e in range(nsub):
        for sub in range(e, nsub):
            step(sub, qi * nsub + e, sub == e)
    o_ref[0] = (acc_ref[...] / l_ref[...]).astype(o_ref.dtype)


def _flash(q, k, v, tq, tk):
    b, hh, s, dk = q.shape
    dv = v.shape[-1]
    assert dv == LANES and tq % tk == 0
    return pl.pallas_call(
        functools.partial(_flash_kernel, tk=tk, nsub=tq // tk),
        grid=(b, hh, s // tq),
        in_specs=[
            pl.BlockSpec((1, 1, tq, dk), lambda bi, h, qi: (bi, h, qi, 0)),
            pl.BlockSpec((1, 1, s, dk), lambda bi, h, qi: (bi, h, 0, 0)),
            pl.BlockSpec((1, 1, s, dv), lambda bi, h, qi: (bi, h, 0, 0)),
        ],
        out_specs=pl.BlockSpec((1, tq, dv), lambda bi, h, qi: (bi, qi, h)),
        out_shape=jax.ShapeDtypeStruct((b, s, hh * dv), BF16),
        scratch_shapes=[pltpu.VMEM((tq, LANES), F32), pltpu.VMEM((tq, LANES), F32), pltpu.VMEM((tq, dv), F32)],
        compiler_params=_cparams(("parallel", "parallel", "arbitrary")),
        name="flash",
    )(q, k, v)


def _rwkv_prep_kernel(z_ref, zp_ref, mu_ref, w0_ref, ww_ref, a0_ref, wa_ref, wg_ref, kk_ref, ka_ref,
                      r_o, k_o, v_o, kk_o, a_o, lw_o, g_o, *, seq):
    tm = z_ref.shape[0]
    z = z_ref[...]
    first = (pl.program_id(0) * tm) % seq == 0
    prow = jnp.where(first, 0.0, zp_ref[7:8, :])
    rowid = lax.broadcasted_iota(jnp.int32, z.shape, 0)
    prev = jnp.where(rowid == 0, prow, pltpu.roll(z, 1, axis=0))
    zs = z + (prev - z) * mu_ref[...]
    d = RWKV_DIM
    r = zs[:, :d]
    k = zs[:, d:2 * d]
    v = zs[:, 2 * d:3 * d]
    wa_lo = zs[:, 3 * d:3 * d + LANES]
    g_lo = zs[:, 3 * d + LANES:]
    wpre = w0_ref[...] + _dot(jnp.tanh(wa_lo).astype(BF16), ww_ref[...])
    y = -wpre
    w = -(jnp.maximum(y, 0.0) + jnp.log(1.0 + jnp.exp(-jnp.abs(y)))) - 0.5
    a = jax.nn.sigmoid(a0_ref[...] + _dot(wa_lo.astype(BF16), wa_ref[...]))
    g = _dot(jax.nn.sigmoid(g_lo).astype(BF16), wg_ref[...])
    r_o[...] = r
    v_o[...] = v
    a_o[...] = a
    g_o[...] = g
    lw_o[...] = -jnp.exp(w)
    k_o[...] = k * (1.0 + (a - 1.0) * ka_ref[...])
    kk = k * kk_ref[...]
    lo = lax.broadcasted_iota(jnp.int32, (tm, LANES), 1) < RWKV_HEAD
    for t in range(d // LANES):
        kt = kk[:, t * LANES:(t + 1) * LANES]
        s0, s1 = _seg2(kt * kt)
        nrm = jnp.where(lo, jnp.maximum(jnp.sqrt(s0), 1e-12), jnp.maximum(jnp.sqrt(s1), 1e-12))
        kk_o[:, t * LANES:(t + 1) * LANES] = kt / nrm


def _rwkv_prep(z, mu, w0, ww, a0, wa, wg, kkp, kap, seq, tm):
    n, zin = z.shape
    d = RWKV_DIM
    full = lambda a: pl.BlockSpec(a.shape, lambda i: (0,) * a.ndim)
    out = pl.BlockSpec((tm, d), lambda i: (i, 0))
    return pl.pallas_call(
        functools.partial(_rwkv_prep_kernel, seq=seq),
        grid=(n // tm,),
        in_specs=[
            pl.BlockSpec((tm, zin), lambda i: (i, 0)),
            pl.BlockSpec((8, zin), lambda i: (jnp.maximum(i * (tm // 8) - 1, 0), 0)),
            full(mu), full(w0), full(ww), full(a0), full(wa), full(wg), full(kkp), full(kap),
        ],
        out_specs=[out] * 7,
        out_shape=[jax.ShapeDtypeStruct((n, d), F32)] * 7,
        compiler_params=_cparams(("parallel",)),
        name="rwkv_prep",
    )(z, z, mu, w0, ww, a0, wa, wg, kkp, kap)


def _split(x):
    hi = x.astype(BF16)
    return hi, (x - hi.astype(F32)).astype(BF16)


def _bmm(a, b, spec, passes):
    mm = lambda u, w: jnp.einsum(spec, u, w, preferred_element_type=F32)
    if passes == 1:
        return mm(a.astype(BF16), b.astype(BF16))
    ah, al = _split(a)
    bh, bl = _split(b)
    return mm(ah, bh) + mm(al, bh) + mm(ah, bl)


def _tiles(x):
    return jnp.stack([x[:, t * LANES:(t + 1) * LANES] for t in range(x.shape[1] // LANES)])


def _rwkv_scan_kernel(r_ref, k_ref, v_ref, kk_ref, a_ref, lw_ref, g_ref, rk_ref, lnw_ref, lnb_ref,
                      o_ref, h_ref, *, passes):
    c = CHUNK
    c2 = 2 * c

    @pl.when(pl.program_id(1) == 0)
    def _():
        h_ref[...] = jnp.zeros_like(h_ref)

    lw = lw_ref[...]
    ti = lax.broadcasted_iota(jnp.int32, (c, c), 0)
    si = lax.broadcasted_iota(jnp.int32, (c, c), 1)
    tril = jnp.where(si <= ti, 1.0, 0.0)
    lwh, lwl = _split(lw)
    lwl2 = (lw - lwh.astype(F32) - lwl.astype(F32)).astype(BF16)
    trilb = tril.astype(BF16)
    cum = _dot(trilb, lwh) + _dot(trilb, lwl) + _dot(trilb, lwl2)
    p = jnp.exp(cum)
    pinv = jnp.exp(-cum)
    pprev = jnp.exp(cum - lw)
    pend = jnp.exp(cum[c - 1:c, :] - cum)
    r = r_ref[...]
    k = k_ref[...]
    v = v_ref[...]
    kk = kk_ref[...]
    b = kk * a_ref[...]
    at = _tiles(-kk * pprev)
    bt = _tiles(b * pinv)
    kt = _tiles(k * pinv)
    rt = _tiles(r * p)
    bh = _tiles(b * pend)
    kh = _tiles(k * pend)
    vt = _tiles(v)
    pc = _tiles(jnp.exp(cum[c - 1:c, :]))

    lo = lax.broadcasted_iota(jnp.int32, (1, 1, LANES), 2) < RWKV_HEAD

    def bd(x):
        return jnp.concatenate([jnp.where(lo, x, 0.0), jnp.where(lo, 0.0, x)], axis=1)

    a_bd, b_bd, k_bd, r_bd, v_bd = bd(at), bd(bt), bd(kt), bd(rt), bd(vt)
    bh_bd, kh_bd = bd(bh), bd(kh)

    amat = _bmm(jnp.concatenate([a_bd, r_bd], axis=1), jnp.concatenate([b_bd, k_bd], axis=1),
                "pil,pjl->pij", passes)
    ri = lax.broadcasted_iota(jnp.int32, (1, c2, c2), 1)
    ci = lax.broadcasted_iota(jnp.int32, (1, c2, c2), 2)
    same = (ri // c) == (ci // c)
    strict = same & ((ri % c) > (ci % c))
    incl = same & ((ri % c) >= (ci % c))
    a_ab = jnp.where(strict, amat[:, :c2, :c2], 0.0)
    a_ak = jnp.where(strict, amat[:, :c2, c2:], 0.0)
    a_rb = jnp.where(incl, amat[:, c2:, :c2], 0.0)
    a_rk = jnp.where(incl, amat[:, c2:, c2:], 0.0)

    eye = jnp.where(ri == ci, 1.0, 0.0)
    tinv = eye + a_ab
    apow = a_ab
    for _ in range(int(math.log2(c)) - 1):
        apow = _bmm(apow, apow, "pij,pjk->pik", passes)
        tinv = tinv + _bmm(tinv, apow, "pij,pjk->pik", passes)

    akv = _bmm(a_ak, v_bd, "pij,pjk->pik", passes)
    wu = _bmm(tinv, jnp.concatenate([a_bd, akv], axis=2), "pij,pjk->pik", passes)
    rhs = jnp.concatenate([wu, jnp.concatenate([jnp.zeros_like(v_bd), v_bd], axis=2)], axis=1)
    top = _bmm(jnp.concatenate([a_rb, a_rk], axis=2), rhs, "pij,pjk->pik", passes)
    bot = _bmm(jnp.concatenate([bh_bd, kh_bd], axis=1), rhs, "psk,psn->pkn", passes)
    q_eff = r_bd + top[:, :, :c2]
    o_intra = top[:, :, c2:]
    m_mat = eye * pc + bot[:, :, :c2]
    g_mat = bot[:, :, c2:]

    h0 = h_ref[...]
    seq_out = _bmm(jnp.concatenate([q_eff, m_mat], axis=1), h0, "pij,pjk->pik", passes)
    o_bd = seq_out[:, :c2, :] + o_intra
    h_ref[...] = seq_out[:, c2:, :] + g_mat
    o = jnp.where(lo, o_bd[:, :c, :], o_bd[:, c:, :])

    rk = rk_ref[...]
    lnw = lnw_ref[...]
    lnb = lnb_ref[...]
    g = g_ref[...]
    lo2 = lax.broadcasted_iota(jnp.int32, (c, LANES), 1) < RWKV_HEAD
    inv_n = 1.0 / RWKV_HEAD
    for t in range(RWKV_DIM // LANES):
        sl = slice(t * LANES, (t + 1) * LANES)
        ot = o[t]
        s0, s1 = _seg2(ot)
        cen = ot - jnp.where(lo2, s0, s1) * inv_n
        q0, q1 = _seg2(cen * cen)
        on = cen * lax.rsqrt(jnp.where(lo2, q0, q1) * inv_n + GN_EPS) * lnw[:, sl] + lnb[:, sl]
        b0, b1 = _seg2(r[:, sl] * k[:, sl] * rk[:, sl])
        on = on + jnp.where(lo2, b0, b1) * v[:, sl]
        o_ref[:, sl] = (on * g[:, sl]).astype(o_ref.dtype)


def _rwkv_scan(r, k, v, kk, a, lw, g, rk, lnw, lnb, batch, seq, passes):
    n, d = r.shape
    nck = seq // CHUNK
    blk = pl.BlockSpec((CHUNK, d), lambda bi, ci: (bi * nck + ci, 0))
    full = lambda x: pl.BlockSpec(x.shape, lambda bi, ci: (0,) * x.ndim)
    return pl.pallas_call(
        functools.partial(_rwkv_scan_kernel, passes=passes),
        grid=(batch, nck),
        in_specs=[blk] * 7 + [full(rk), full(lnw), full(lnb)],
        out_specs=blk,
        out_shape=jax.ShapeDtypeStruct((n, d), BF16),
        scratch_shapes=[pltpu.VMEM((d // LANES, LANES, LANES), F32)],
        compiler_params=_cparams(("parallel", "arbitrary")),
        name="rwkv_scan",
    )(r, k, v, kk, a, lw, g, rk, lnw, lnb)


def _out_proj_kernel(x_ref, oa_ref, ob_ref, wa_ref, wb_ref, o_ref):
    o_ref[...] = x_ref[...] + _dot(oa_ref[...], wa_ref[...]) + _dot(ob_ref[...], wb_ref[...])


def _out_proj(x, oa, ob, wa, wb, tm, tn):
    n, d = x.shape
    ka = oa.shape[1]
    kb = ob.shape[1]
    return pl.pallas_call(
        _out_proj_kernel,
        grid=(n // tm, d // tn),
        in_specs=[
            pl.BlockSpec((tm, tn), lambda i, j: (i, j)),
            pl.BlockSpec((tm, ka), lambda i, j: (i, 0)),
            pl.BlockSpec((tm, kb), lambda i, j: (i, 0)),
            pl.BlockSpec((ka, tn), lambda i, j: (0, j)),
            pl.BlockSpec((kb, tn), lambda i, j: (0, j)),
        ],
        out_specs=pl.BlockSpec((tm, tn), lambda i, j: (i, j)),
        out_shape=jax.ShapeDtypeStruct((n, d), F32),
        compiler_params=_cparams(("parallel", "parallel")),
        name="out_proj",
    )(x, oa, ob, wa, wb)


def _ple_kernel(x_ref, g_ref, p_ref, wg_ref, wp_ref, o_ref, h_ref, *, tn):
    j = pl.program_id(1)

    @pl.when(j == 0)
    def _():
        h_ref[...] = _rms(x_ref[...], g_ref[...]).astype(BF16)

    gate = jax.nn.sigmoid(_dot(h_ref[...], wg_ref[...]))
    emb = _dot(p_ref[...].astype(BF16), wp_ref[...])
    col = pl.multiple_of(j * tn, LANES)
    o_ref[...] = x_ref[:, pl.ds(col, tn)] + gate * emb


def _ple(x, g, p, wg, wp, tm, tn):
    n, d = x.shape
    pd = p.shape[1]
    return pl.pallas_call(
        functools.partial(_ple_kernel, tn=tn),
        grid=(n // tm, d // tn),
        in_specs=[
            pl.BlockSpec((tm, d), lambda i, j: (i, 0)),
            pl.BlockSpec((1, d), lambda i, j: (0, 0)),
            pl.BlockSpec((tm, pd), lambda i, j: (i, 0)),
            pl.BlockSpec((d, tn), lambda i, j: (0, j)),
            pl.BlockSpec((pd, tn), lambda i, j: (0, j)),
        ],
        out_specs=pl.BlockSpec((tm, tn), lambda i, j: (i, j)),
        out_shape=jax.ShapeDtypeStruct((n, d), F32),
        scratch_shapes=[pltpu.VMEM((tm, d), BF16)],
        compiler_params=_cparams(("parallel", "arbitrary")),
        name="ple",
    )(x, g, p, wg, wp)


def _tile(n, pref):
    t = min(n, pref)
    assert n % t == 0, (n, t)
    return t


def _layer(x, p, norm_ffn1, w1_gate, w1_up, w1_down, norm_mix, w_in, q_a_norm, w_q_b, kv_a_norm, w_kv_b,
           q_norm, k_norm, mu_shift, w0, w_w2, a0, w_a2, w_g2, k_k, k_a, r_k, ln_x_w, ln_x_b, w_out,
           norm_ffn2, w2_gate, w2_up, w2_down, norm_ple, w_ple_gate, w_ple_proj):
    batch, seq, d = x.shape
    n = batch * seq
    row = lambda a: a.reshape(1, -1)
    bf = lambda a: a.astype(BF16)
    mla_in = Q_LORA + KV_LORA + QK_ROPE
    dff = w1_gate.shape[1]
    tf = _tile(dff, 512)
    tm_ffn = _tile(n, 512)

    xf = x.reshape(n, d)
    x1 = _ffn(xf, row(norm_ffn1), bf(w1_gate), bf(w1_up), bf(w1_down), tm_ffn, tf)

    w_mla = bf(jnp.concatenate([w_in[:, :mla_in], w_in[:, mla_in - QK_ROPE:mla_in]], axis=1))
    wq = w_q_b.reshape(Q_LORA, MLA_HEADS, QK_HEAD)
    wqb = bf(jnp.concatenate([wq[:, :, :QK_NOPE].reshape(Q_LORA, -1), wq[:, :, QK_NOPE:].reshape(Q_LORA, -1)], axis=1))
    inv_freq = 1.0 / (ROPE_BASE ** (jnp.arange(0, QK_ROPE, 2, dtype=F32) / QK_ROPE))
    ang = jnp.arange(seq, dtype=F32)[:, None] * inv_freq[None, :]
    cos, sin = jnp.cos(ang), jnp.sin(ang)
    cos_t = jnp.concatenate([cos, cos, cos, cos], axis=1)
    sin_t = jnp.concatenate([-sin, sin, -sin, sin], axis=1)
    pair = lambda a: row(jnp.concatenate([a, a]))
    tm_prep = _tile(seq, 256)
    q, k, v = _mla_prep(x1, row(norm_mix), w_mla, row(q_a_norm), wqb, row(kv_a_norm), bf(w_kv_b),
                        row(q_norm[:QK_NOPE]), pair(q_norm[QK_NOPE:]), row(k_norm[:QK_NOPE]),
                        pair(k_norm[QK_NOPE:]), cos_t, sin_t, batch, seq, tm_prep)
    o_mla = _flash(q, k, v, _tile(seq, 1024), _tile(seq, 512)).reshape(n, MLA_HEADS * V_HEAD)

    z = _norm_mm(x1, row(norm_mix), bf(w_in[:, mla_in:]), _tile(n, 512), 13 * LANES)
    zero = jnp.zeros_like(w_w2)
    ww = bf(jnp.concatenate([w_w2, zero], axis=0))
    wa = bf(jnp.concatenate([zero, w_a2], axis=0))
    rw = _rwkv_prep(z, row(mu_shift), row(w0), ww, row(a0), wa, bf(w_g2), row(k_k), row(k_a), seq, tm_prep)
    o_rwkv = _rwkv_scan(*rw, row(r_k), row(ln_x_w), row(ln_x_b), batch, seq, 3)

    half = MLA_HEADS * V_HEAD
    x2 = _out_proj(x1, o_mla, o_rwkv, bf(w_out[:half]), bf(w_out[half:]), _tile(n, 512), _tile(d, 1024))
    x3 = _ffn(x2, row(norm_ffn2), bf(w2_gate), bf(w2_up), bf(w2_down), tm_ffn, tf)
    out = _ple(x3, row(norm_ple), p.reshape(n, -1), bf(w_ple_gate), bf(w_ple_proj), _tile(n, 512), _tile(d, 1024))
    return out.reshape(batch, seq, d)


def kernel(x, p, norm_ffn1, w1_gate, w1_up, w1_down, norm_mix, w_in, q_a_norm, w_q_b, kv_a_norm, w_kv_b, q_norm, k_norm, mu_shift, w0, w_w2, a0, w_a2, w_g2, k_k, k_a, r_k, ln_x_w, ln_x_b, w_out, norm_ffn2, w2_gate, w2_up, w2_down, norm_ple, w_ple_gate, w_ple_proj):
    depth = p.shape[0]
    for i in range(depth):
        x = _layer(x, p[i], norm_ffn1[i], w1_gate[i], w1_up[i], w1_down[i], norm_mix[i], w_in[i], q_a_norm[i],
                   w_q_b[i], kv_a_norm[i], w_kv_b[i], q_norm[i], k_norm[i], mu_shift[i], w0[i], w_w2[i], a0[i],
                   w_a2[i], w_g2[i], k_k[i], k_a[i], r_k[i], ln_x_w[i], ln_x_b[i], w_out[i], norm_ffn2[i],
                   w2_gate[i], w2_up[i], w2_down[i], norm_ple[i], w_ple_gate[i], w_ple_proj[i])
    return x
```

```python
import functools
import math

import jax
import jax.numpy as jnp
from jax import lax
from jax.experimental import pallas as pl
from jax.experimental.pallas import tpu as pltpu

F32 = jnp.float32
BF16 = jnp.bfloat16

EPS = 1e-6
GN_EPS = 64e-5
MLA_HEADS = 8
QK_NOPE = 128
QK_ROPE = 64
QK_HEAD = QK_NOPE + QK_ROPE
V_HEAD = 128
Q_LORA = 512
KV_LORA = 256
ROPE_BASE = 10000.0
RWKV_HEAD = 64
RWKV_HEADS = 16
RWKV_DIM = RWKV_HEADS * RWKV_HEAD
LANES = 128
CHUNK = 64
VMEM_LIMIT = 56 * 1024 * 1024
SCAN_PASSES = (1, 1, 1, 1, 1, 1)


def _cparams(sem):
    return pltpu.CompilerParams(dimension_semantics=sem, vmem_limit_bytes=VMEM_LIMIT)


def _dot(a, b):
    return jnp.dot(a, b, preferred_element_type=F32)


def _rms(x, g):
    return x * lax.rsqrt(jnp.mean(x * x, axis=-1, keepdims=True) + EPS) * g


def _ffn_kernel(x_ref, g_ref, wg_ref, wu_ref, wd_ref, o_ref, h_ref, acc_ref):
    j = pl.program_id(1)

    @pl.when(j == 0)
    def _():
        h_ref[...] = _rms(x_ref[...], g_ref[...]).astype(BF16)
        acc_ref[...] = jnp.zeros_like(acc_ref)

    h = h_ref[...]
    gate = _dot(h, wg_ref[...])
    up = _dot(h, wu_ref[...])
    act = (gate * jax.nn.sigmoid(gate) * up).astype(BF16)
    acc_ref[...] += _dot(act, wd_ref[...])

    @pl.when(j == pl.num_programs(1) - 1)
    def _():
        o_ref[...] = x_ref[...] + 0.5 * acc_ref[...]


def _ffn(x, g, wg, wu, wd, tm, tf):
    n, d = x.shape
    dff = wg.shape[1]
    return pl.pallas_call(
        _ffn_kernel,
        grid=(n // tm, dff // tf),
        in_specs=[
            pl.BlockSpec((tm, d), lambda i, j: (i, 0)),
            pl.BlockSpec((1, d), lambda i, j: (0, 0)),
            pl.BlockSpec((d, tf), lambda i, j: (0, j)),
            pl.BlockSpec((d, tf), lambda i, j: (0, j)),
            pl.BlockSpec((tf, d), lambda i, j: (j, 0)),
        ],
        out_specs=pl.BlockSpec((tm, d), lambda i, j: (i, 0)),
        out_shape=jax.ShapeDtypeStruct((n, d), F32),
        scratch_shapes=[pltpu.VMEM((tm, d), BF16), pltpu.VMEM((tm, d), F32)],
        compiler_params=_cparams(("parallel", "arbitrary")),
        name="ffn",
    )(x, g, wg, wu, wd)


def _norm_mm_kernel(x_ref, g_ref, w_ref, o_ref, h_ref):
    @pl.when(pl.program_id(1) == 0)
    def _():
        h_ref[...] = _rms(x_ref[...], g_ref[...]).astype(BF16)

    o_ref[...] = _dot(h_ref[...], w_ref[...])


def _norm_mm(x, g, w, tm, tn):
    n, d = x.shape
    nc = w.shape[1]
    return pl.pallas_call(
        _norm_mm_kernel,
        grid=(n // tm, nc // tn),
        in_specs=[
            pl.BlockSpec((tm, d), lambda i, j: (i, 0)),
            pl.BlockSpec((1, d), lambda i, j: (0, 0)),
            pl.BlockSpec((d, tn), lambda i, j: (0, j)),
        ],
        out_specs=pl.BlockSpec((tm, tn), lambda i, j: (i, j)),
        out_shape=jax.ShapeDtypeStruct((n, nc), F32),
        scratch_shapes=[pltpu.VMEM((tm, d), BF16)],
        compiler_params=_cparams(("parallel", "arbitrary")),
        name="norm_mm",
    )(x, g, w)


def _seg2(x):
    lo = lax.broadcasted_iota(jnp.int32, x.shape, 1) < RWKV_HEAD
    s0 = jnp.sum(jnp.where(lo, x, 0.0), axis=-1, keepdims=True)
    s1 = jnp.sum(jnp.where(lo, 0.0, x), axis=-1, keepdims=True)
    return s0, s1


def _rope_pair(y, cos_t, sin_t):
    lane = lax.broadcasted_iota(jnp.int32, y.shape, 1)
    first = (lane % QK_ROPE) < (QK_ROPE // 2)
    rot = jnp.where(first, pltpu.roll(y, LANES - QK_ROPE // 2, axis=1), pltpu.roll(y, QK_ROPE // 2, axis=1))
    return y * cos_t + rot * sin_t


def _mla_prep_kernel(x_ref, gmix_ref, win_ref, qag_ref, wqb_ref, kvag_ref, wkvb_ref,
                     gqn_ref, gqp_ref, gkn_ref, gkp_ref, cos_ref, sin_ref,
                     q_ref, k_ref, v_ref):
    scale = 1.0 / math.sqrt(QK_HEAD)
    h = _rms(x_ref[...], gmix_ref[...]).astype(BF16)
    z = _dot(h, win_ref[...])
    q_lat = z[:, :Q_LORA]
    kv_lat = z[:, Q_LORA:Q_LORA + KV_LORA]
    kpe2 = z[:, Q_LORA + KV_LORA:]
    qf = _dot(_rms(q_lat, qag_ref[...]).astype(BF16), wqb_ref[...])
    kvf = _dot(_rms(kv_lat, kvag_ref[...]).astype(BF16), wkvb_ref[...])
    cos_t = cos_ref[...]
    sin_t = sin_ref[...]
    lo = lax.broadcasted_iota(jnp.int32, cos_t.shape, 1) < QK_ROPE
    kpe_ss, _ = _seg2(kpe2 * kpe2)
    nope_w = MLA_HEADS * QK_NOPE
    for j in range(MLA_HEADS // 2):
        qp = qf[:, nope_w + j * LANES: nope_w + (j + 1) * LANES]
        qs0, qs1 = _seg2(qp * qp)
        rs_q = []
        rs_k = []
        for e, qs in ((0, qs0), (1, qs1)):
            hd = 2 * j + e
            qn = qf[:, hd * QK_NOPE:(hd + 1) * QK_NOPE]
            rq = lax.rsqrt((jnp.sum(qn * qn, axis=-1, keepdims=True) + qs) * (1.0 / QK_HEAD) + EPS)
            q_ref[0, hd, :, :QK_NOPE] = (qn * rq * gqn_ref[...] * scale).astype(BF16)
            rs_q.append(rq)
            kn = kvf[:, hd * 2 * QK_NOPE: hd * 2 * QK_NOPE + QK_NOPE]
            rk = lax.rsqrt((jnp.sum(kn * kn, axis=-1, keepdims=True) + kpe_ss) * (1.0 / QK_HEAD) + EPS)
            k_ref[0, hd, :, :QK_NOPE] = (kn * rk * gkn_ref[...]).astype(BF16)
            rs_k.append(rk)
            v_ref[0, hd] = kvf[:, hd * 2 * QK_NOPE + QK_NOPE:(hd + 1) * 2 * QK_NOPE].astype(BF16)
        yq = _rope_pair(qp * jnp.where(lo, rs_q[0], rs_q[1]) * gqp_ref[...], cos_t, sin_t) * scale
        yk = _rope_pair(kpe2 * jnp.where(lo, rs_k[0], rs_k[1]) * gkp_ref[...], cos_t, sin_t)
        q_ref[0, 2 * j, :, QK_NOPE:] = jnp.where(lo, yq, 0.0).astype(BF16)
        q_ref[0, 2 * j + 1, :, QK_NOPE:] = jnp.where(lo, 0.0, yq).astype(BF16)
        k_ref[0, 2 * j, :, QK_NOPE:] = jnp.where(lo, yk, 0.0).astype(BF16)
        k_ref[0, 2 * j + 1, :, QK_NOPE:] = jnp.where(lo, 0.0, yk).astype(BF16)


def _mla_prep(x, gmix, win, qag, wqb, kvag, wkvb, gqn, gqp, gkn, gkp, cos_t, sin_t, batch, seq, tm):
    n, d = x.shape
    spt = seq // tm
    full = lambda a: pl.BlockSpec(a.shape, lambda i: (0,) * a.ndim)
    qk_w = 2 * LANES
    return pl.pallas_call(
        _mla_prep_kernel,
        grid=(n // tm,),
        in_specs=[
            pl.BlockSpec((tm, d), lambda i: (i, 0)),
            full(gmix), full(win), full(qag), full(wqb), full(kvag), full(wkvb),
            full(gqn), full(gqp), full(gkn), full(gkp),
            pl.BlockSpec((tm, LANES), lambda i: (i % spt, 0)),
            pl.BlockSpec((tm, LANES), lambda i: (i % spt, 0)),
        ],
        out_specs=[
            pl.BlockSpec((1, MLA_HEADS, tm, qk_w), lambda i: (i // spt, 0, i % spt, 0)),
            pl.BlockSpec((1, MLA_HEADS, tm, qk_w), lambda i: (i // spt, 0, i % spt, 0)),
            pl.BlockSpec((1, MLA_HEADS, tm, V_HEAD), lambda i: (i // spt, 0, i % spt, 0)),
        ],
        out_shape=[
            jax.ShapeDtypeStruct((batch, MLA_HEADS, seq, qk_w), BF16),
            jax.ShapeDtypeStruct((batch, MLA_HEADS, seq, qk_w), BF16),
            jax.ShapeDtypeStruct((batch, MLA_HEADS, seq, V_HEAD), BF16),
        ],
        compiler_params=_cparams(("parallel",)),
        name="mla_prep",
    )(x, gmix, win, qag, wqb, kvag, wkvb, gqn, gqp, gkn, gkp, cos_t, sin_t)


def _flash_kernel(q_ref, k_ref, v_ref, o_ref, m_ref, l_ref, acc_ref, *, tk, nsub):
    qi = pl.program_id(2)
    m_ref[...] = jnp.full_like(m_ref, -1e30)
    l_ref[...] = jnp.zeros_like(l_ref)
    acc_ref[...] = jnp.zeros_like(acc_ref)
    rep = tk // LANES
    causal = lax.broadcasted_iota(jnp.int32, (tk, tk), 1) <= lax.broadcasted_iota(jnp.int32, (tk, tk), 0)

    def step(sub, j, diag):
        rows = pl.ds(sub * tk, tk)
        kv = pl.ds(pl.multiple_of(j * tk, tk), tk)
        s = lax.dot_general(q_ref[0, 0, rows, :], k_ref[0, 0, kv, :], (((1,), (1,)), ((), ())),
                            preferred_element_type=F32)
        if diag:
            s = jnp.where(causal, s, -1e30)
        m_prev = m_ref[rows, :]
        m_new = jnp.maximum(m_prev, jnp.max(s, axis=-1, keepdims=True))
        alpha = jnp.exp(m_prev - m_new)
        p = jnp.exp(s - jnp.concatenate([m_new] * rep, axis=1))
        l_ref[rows, :] = alpha * l_ref[rows, :] + jnp.sum(p, axis=-1, keepdims=True)
        acc_ref[rows, :] = alpha * acc_ref[rows, :] + _dot(p.astype(BF16), v_ref[0, 0, kv, :])
        m_ref[rows, :] = m_new

    def body(j, carry):
        for sub in range(nsub):
            step(sub, j, False)
        return carry

    lax.fori_loop(0, qi * nsub, body, 0)
    for e in range(nsub):
        for sub in range(e, nsub):
            step(sub, qi * nsub + e, sub == e)
    o_ref[0] = (acc_ref[...] / l_ref[...]).astype(o_ref.dtype)


def _flash(q, k, v, tq, tk):
    b, hh, s, dk = q.shape
    dv = v.shape[-1]
    assert dv == LANES and tq % tk == 0
    return pl.pallas_call(
        functools.partial(_flash_kernel, tk=tk, nsub=tq // tk),
        grid=(b, hh, s // tq),
        in_specs=[
            pl.BlockSpec((1, 1, tq, dk), lambda bi, h, qi: (bi, h, qi, 0)),
            pl.BlockSpec((1, 1, s, dk), lambda bi, h, qi: (bi, h, 0, 0)),
            pl.BlockSpec((1, 1, s, dv), lambda bi, h, qi: (bi, h, 0, 0)),
        ],
        out_specs=pl.BlockSpec((1, tq, dv), lambda bi, h, qi: (bi, qi, h)),
        out_shape=jax.ShapeDtypeStruct((b, s, hh * dv), BF16),
        scratch_shapes=[pltpu.VMEM((tq, LANES), F32), pltpu.VMEM((tq, LANES), F32), pltpu.VMEM((tq, dv), F32)],
        compiler_params=_cparams(("parallel", "parallel", "arbitrary")),
        name="flash",
    )(q, k, v)


def _rwkv_prep_kernel(z_ref, zp_ref, mu_ref, w0_ref, ww_ref, a0_ref, wa_ref, wg_ref, kk_ref, ka_ref,
                      r_o, k_o, v_o, kk_o, a_o, lw_o, g_o, *, seq):
    tm = z_ref.shape[0]
    z = z_ref[...]
    first = (pl.program_id(0) * tm) % seq == 0
    prow = jnp.where(first, 0.0, zp_ref[7:8, :])
    rowid = lax.broadcasted_iota(jnp.int32, z.shape, 0)
    prev = jnp.where(rowid == 0, prow, pltpu.roll(z, 1, axis=0))
    zs = z + (prev - z) * mu_ref[...]
    d = RWKV_DIM
    r = zs[:, :d]
    k = zs[:, d:2 * d]
    v = zs[:, 2 * d:3 * d]
    wa_lo = zs[:, 3 * d:3 * d + LANES]
    g_lo = zs[:, 3 * d + LANES:]
    wpre = w0_ref[...] + _dot(jnp.tanh(wa_lo).astype(BF16), ww_ref[...])
    y = -wpre
    w = -(jnp.maximum(y, 0.0) + jnp.log(1.0 + jnp.exp(-jnp.abs(y)))) - 0.5
    a = jax.nn.sigmoid(a0_ref[...] + _dot(wa_lo.astype(BF16), wa_ref[...]))
    g = _dot(jax.nn.sigmoid(g_lo).astype(BF16), wg_ref[...])
    r_o[...] = r
    v_o[...] = v
    a_o[...] = a
    g_o[...] = g
    lw_o[...] = -jnp.exp(w)
    k_o[...] = k * (1.0 + (a - 1.0) * ka_ref[...])
    kk = k * kk_ref[...]
    lo = lax.broadcasted_iota(jnp.int32, (tm, LANES), 1) < RWKV_HEAD
    for t in range(d // LANES):
        kt = kk[:, t * LANES:(t + 1) * LANES]
        s0, s1 = _seg2(kt * kt)
        nrm = jnp.where(lo, jnp.maximum(jnp.sqrt(s0), 1e-12), jnp.maximum(jnp.sqrt(s1), 1e-12))
        kk_o[:, t * LANES:(t + 1) * LANES] = kt / nrm


def _rwkv_prep(z, mu, w0, ww, a0, wa, wg, kkp, kap, seq, tm):
    n, zin = z.shape
    d = RWKV_DIM
    full = lambda a: pl.BlockSpec(a.shape, lambda i: (0,) * a.ndim)
    out = pl.BlockSpec((tm, d), lambda i: (i, 0))
    return pl.pallas_call(
        functools.partial(_rwkv_prep_kernel, seq=seq),
        grid=(n // tm,),
        in_specs=[
            pl.BlockSpec((tm, zin), lambda i: (i, 0)),
            pl.BlockSpec((8, zin), lambda i: (jnp.maximum(i * (tm // 8) - 1, 0), 0)),
            full(mu), full(w0), full(ww), full(a0), full(wa), full(wg), full(kkp), full(kap),
        ],
        out_specs=[out] * 7,
        out_shape=[jax.ShapeDtypeStruct((n, d), F32)] * 7,
        compiler_params=_cparams(("parallel",)),
        name="rwkv_prep",
    )(z, z, mu, w0, ww, a0, wa, wg, kkp, kap)


def _split(x):
    hi = x.astype(BF16)
    return hi, (x - hi.astype(F32)).astype(BF16)


def _bmm(a, b, spec, passes):
    mm = lambda u, w: jnp.einsum(spec, u, w, preferred_element_type=F32)
    if passes == 1:
        return mm(a.astype(BF16), b.astype(BF16))
    ah, al = _split(a)
    bh, bl = _split(b)
    return mm(ah, bh) + mm(al, bh) + mm(ah, bl)


def _tiles(x):
    return jnp.stack([x[:, t * LANES:(t + 1) * LANES] for t in range(x.shape[1] // LANES)])


def _rwkv_scan_kernel(r_ref, k_ref, v_ref, kk_ref, a_ref, lw_ref, g_ref, rk_ref, lnw_ref, lnb_ref,
                      o_ref, h_ref, *, passes):
    c = CHUNK
    c2 = 2 * c

    @pl.when(pl.program_id(1) == 0)
    def _():
        h_ref[...] = jnp.zeros_like(h_ref)

    lw = lw_ref[...]
    ti = lax.broadcasted_iota(jnp.int32, (c, c), 0)
    si = lax.broadcasted_iota(jnp.int32, (c, c), 1)
    tril = jnp.where(si <= ti, 1.0, 0.0)
    lwh, lwl = _split(lw)
    lwl2 = (lw - lwh.astype(F32) - lwl.astype(F32)).astype(BF16)
    trilb = tril.astype(BF16)
    cum = _dot(trilb, lwh) + _dot(trilb, lwl) + _dot(trilb, lwl2)
    p = jnp.exp(cum)
    pinv = jnp.exp(-cum)
    pprev = jnp.exp(cum - lw)
    pend = jnp.exp(cum[c - 1:c, :] - cum)
    r = r_ref[...]
    k = k_ref[...]
    v = v_ref[...]
    kk = kk_ref[...]
    b = kk * a_ref[...]
    at = _tiles(-kk * pprev)
    bt = _tiles(b * pinv)
    kt = _tiles(k * pinv)
    rt = _tiles(r * p)
    bh = _tiles(b * pend)
    kh = _tiles(k * pend)
    vt = _tiles(v)
    pc = _tiles(jnp.exp(cum[c - 1:c, :]))

    lo = lax.broadcasted_iota(jnp.int32, (1, 1, LANES), 2) < RWKV_HEAD

    def bd(x):
        return jnp.concatenate([jnp.where(lo, x, 0.0), jnp.where(lo, 0.0, x)], axis=1)

    a_bd, b_bd, k_bd, r_bd, v_bd = bd(at), bd(bt), bd(kt), bd(rt), bd(vt)
    bh_bd, kh_bd = bd(bh), bd(kh)

    amat = _bmm(jnp.concatenate([a_bd, r_bd], axis=1), jnp.concatenate([b_bd, k_bd], axis=1),
                "pil,pjl->pij", passes[0])
    ri = lax.broadcasted_iota(jnp.int32, (1, c2, c2), 1)
    ci = lax.broadcasted_iota(jnp.int32, (1, c2, c2), 2)
    same = (ri // c) == (ci // c)
    strict = same & ((ri % c) > (ci % c))
    incl = same & ((ri % c) >= (ci % c))
    a_ab = jnp.where(strict, amat[:, :c2, :c2], 0.0)
    a_ak = jnp.where(strict, amat[:, :c2, c2:], 0.0)
    a_rb = jnp.where(incl, amat[:, c2:, :c2], 0.0)
    a_rk = jnp.where(incl, amat[:, c2:, c2:], 0.0)

    eye = jnp.where(ri == ci, 1.0, 0.0)
    tinv = eye + a_ab
    apow = a_ab
    for _ in range(int(math.log2(c)) - 1):
        apow = _bmm(apow, apow, "pij,pjk->pik", passes[1])
        tinv = tinv + _bmm(tinv, apow, "pij,pjk->pik", passes[1])

    akv = _bmm(a_ak, v_bd, "pij,pjk->pik", passes[2])
    wu = _bmm(tinv, jnp.concatenate([a_bd, akv], axis=2), "pij,pjk->pik", passes[3])
    rhs = jnp.concatenate([wu, jnp.concatenate([jnp.zeros_like(v_bd), v_bd], axis=2)], axis=1)
    top = _bmm(jnp.concatenate([a_rb, a_rk], axis=2), rhs, "pij,pjk->pik", passes[4])
    bot = _bmm(jnp.concatenate([bh_bd, kh_bd], axis=1), rhs, "psk,psn->pkn", passes[4])
    q_eff = r_bd + top[:, :, :c2]
    o_intra = top[:, :, c2:]
    m_mat = eye * pc + bot[:, :, :c2]
    g_mat = bot[:, :, c2:]

    h0 = h_ref[...]
    seq_out = _bmm(jnp.concatenate([q_eff, m_mat], axis=1), h0, "pij,pjk->pik", passes[5])
    o_bd = seq_out[:, :c2, :] + o_intra
    h_ref[...] = seq_out[:, c2:, :] + g_mat
    o = jnp.where(lo, o_bd[:, :c, :], o_bd[:, c:, :])

    rk = rk_ref[...]
    lnw = lnw_ref[...]
    lnb = lnb_ref[...]
    g = g_ref[...]
    lo2 = lax.broadcasted_iota(jnp.int32, (c, LANES), 1) < RWKV_HEAD
    inv_n = 1.0 / RWKV_HEAD
    for t in range(RWKV_DIM // LANES):
        sl = slice(t * LANES, (t + 1) * LANES)
        ot = o[t]
        s0, s1 = _seg2(ot)
        cen = ot - jnp.where(lo2, s0, s1) * inv_n
        q0, q1 = _seg2(cen * cen)
        on = cen * lax.rsqrt(jnp.where(lo2, q0, q1) * inv_n + GN_EPS) * lnw[:, sl] + lnb[:, sl]
        b0, b1 = _seg2(r[:, sl] * k[:, sl] * rk[:, sl])
        on = on + jnp.where(lo2, b0, b1) * v[:, sl]
        o_ref[:, sl] = (on * g[:, sl]).astype(o_ref.dtype)


def _rwkv_scan(r, k, v, kk, a, lw, g, rk, lnw, lnb, batch, seq, passes):
    n, d = r.shape
    nck = seq // CHUNK
    blk = pl.BlockSpec((CHUNK, d), lambda bi, ci: (bi * nck + ci, 0))
    full = lambda x: pl.BlockSpec(x.shape, lambda bi, ci: (0,) * x.ndim)
    return pl.pallas_call(
        functools.partial(_rwkv_scan_kernel, passes=passes),
        grid=(batch, nck),
        in_specs=[blk] * 7 + [full(rk), full(lnw), full(lnb)],
        out_specs=blk,
        out_shape=jax.ShapeDtypeStruct((n, d), BF16),
        scratch_shapes=[pltpu.VMEM((d // LANES, LANES, LANES), F32)],
        compiler_params=_cparams(("parallel", "arbitrary")),
        name="rwkv_scan",
    )(r, k, v, kk, a, lw, g, rk, lnw, lnb)


def _out_proj_kernel(x_ref, oa_ref, ob_ref, wa_ref, wb_ref, o_ref):
    o_ref[...] = x_ref[...] + _dot(oa_ref[...], wa_ref[...]) + _dot(ob_ref[...], wb_ref[...])


def _out_proj(x, oa, ob, wa, wb, tm, tn):
    n, d = x.shape
    ka = oa.shape[1]
    kb = ob.shape[1]
    return pl.pallas_call(
        _out_proj_kernel,
        grid=(n // tm, d // tn),
        in_specs=[
            pl.BlockSpec((tm, tn), lambda i, j: (i, j)),
            pl.BlockSpec((tm, ka), lambda i, j: (i, 0)),
            pl.BlockSpec((tm, kb), lambda i, j: (i, 0)),
            pl.BlockSpec((ka, tn), lambda i, j: (0, j)),
            pl.BlockSpec((kb, tn), lambda i, j: (0, j)),
        ],
        out_specs=pl.BlockSpec((tm, tn), lambda i, j: (i, j)),
        out_shape=jax.ShapeDtypeStruct((n, d), F32),
        compiler_params=_cparams(("parallel", "parallel")),
        name="out_proj",
    )(x, oa, ob, wa, wb)


def _ple_kernel(x_ref, g_ref, p_ref, wg_ref, wp_ref, o_ref, h_ref, *, tn):
    j = pl.program_id(1)

    @pl.when(j == 0)
    def _():
        h_ref[...] = _rms(x_ref[...], g_ref[...]).astype(BF16)

    gate = jax.nn.sigmoid(_dot(h_ref[...], wg_ref[...]))
    emb = _dot(p_ref[...].astype(BF16), wp_ref[...])
    col = pl.multiple_of(j * tn, LANES)
    o_ref[...] = x_ref[:, pl.ds(col, tn)] + gate * emb


def _ple(x, g, p, wg, wp, tm, tn):
    n, d = x.shape
    pd = p.shape[1]
    return pl.pallas_call(
        functools.partial(_ple_kernel, tn=tn),
        grid=(n // tm, d // tn),
        in_specs=[
            pl.BlockSpec((tm, d), lambda i, j: (i, 0)),
            pl.BlockSpec((1, d), lambda i, j: (0, 0)),
            pl.BlockSpec((tm, pd), lambda i, j: (i, 0)),
            pl.BlockSpec((d, tn), lambda i, j: (0, j)),
            pl.BlockSpec((pd, tn), lambda i, j: (0, j)),
        ],
        out_specs=pl.BlockSpec((tm, tn), lambda i, j: (i, j)),
        out_shape=jax.ShapeDtypeStruct((n, d), F32),
        scratch_shapes=[pltpu.VMEM((tm, d), BF16)],
        compiler_params=_cparams(("parallel", "arbitrary")),
        name="ple",
    )(x, g, p, wg, wp)


def _tile(n, pref):
    t = min(n, pref)
    assert n % t == 0, (n, t)
    return t


def _layer(x, p, norm_ffn1, w1_gate, w1_up, w1_down, norm_mix, w_in, q_a_norm, w_q_b, kv_a_norm, w_kv_b,
           q_norm, k_norm, mu_shift, w0, w_w2, a0, w_a2, w_g2, k_k, k_a, r_k, ln_x_w, ln_x_b, w_out,
           norm_ffn2, w2_gate, w2_up, w2_down, norm_ple, w_ple_gate, w_ple_proj):
    batch, seq, d = x.shape
    n = batch * seq
    row = lambda a: a.reshape(1, -1)
    bf = lambda a: a.astype(BF16)
    mla_in = Q_LORA + KV_LORA + QK_ROPE
    dff = w1_gate.shape[1]
    tf = _tile(dff, 512)
    tm_ffn = _tile(n, 512)

    xf = x.reshape(n, d)
    x1 = _ffn(xf, row(norm_ffn1), bf(w1_gate), bf(w1_up), bf(w1_down), tm_ffn, tf)

    w_mla = bf(jnp.concatenate([w_in[:, :mla_in], w_in[:, mla_in - QK_ROPE:mla_in]], axis=1))
    wq = w_q_b.reshape(Q_LORA, MLA_HEADS, QK_HEAD)
    wqb = bf(jnp.concatenate([wq[:, :, :QK_NOPE].reshape(Q_LORA, -1), wq[:, :, QK_NOPE:].reshape(Q_LORA, -1)], axis=1))
    inv_freq = 1.0 / (ROPE_BASE ** (jnp.arange(0, QK_ROPE, 2, dtype=F32) / QK_ROPE))
    ang = jnp.arange(seq, dtype=F32)[:, None] * inv_freq[None, :]
    cos, sin = jnp.cos(ang), jnp.sin(ang)
    cos_t = jnp.concatenate([cos, cos, cos, cos], axis=1)
    sin_t = jnp.concatenate([-sin, sin, -sin, sin], axis=1)
    pair = lambda a: row(jnp.concatenate([a, a]))
    tm_prep = _tile(seq, 256)
    q, k, v = _mla_prep(x1, row(norm_mix), w_mla, row(q_a_norm), wqb, row(kv_a_norm), bf(w_kv_b),
                        row(q_norm[:QK_NOPE]), pair(q_norm[QK_NOPE:]), row(k_norm[:QK_NOPE]),
                        pair(k_norm[QK_NOPE:]), cos_t, sin_t, batch, seq, tm_prep)
    o_mla = _flash(q, k, v, _tile(seq, 1024), _tile(seq, 512)).reshape(n, MLA_HEADS * V_HEAD)

    z = _norm_mm(x1, row(norm_mix), bf(w_in[:, mla_in:]), _tile(n, 512), 13 * LANES)
    zero = jnp.zeros_like(w_w2)
    ww = bf(jnp.concatenate([w_w2, zero], axis=0))
    wa = bf(jnp.concatenate([zero, w_a2], axis=0))
    rw = _rwkv_prep(z, row(mu_shift), row(w0), ww, row(a0), wa, bf(w_g2), row(k_k), row(k_a), seq, tm_prep)
    o_rwkv = _rwkv_scan(*rw, row(r_k), row(ln_x_w), row(ln_x_b), batch, seq, SCAN_PASSES)

    half = MLA_HEADS * V_HEAD
    x2 = _out_proj(x1, o_mla, o_rwkv, bf(w_out[:half]), bf(w_out[half:]), _tile(n, 512), _tile(d, 1024))
    x3 = _ffn(x2, row(norm_ffn2), bf(w2_gate), bf(w2_up), bf(w2_down), tm_ffn, tf)
    out = _ple(x3, row(norm_ple), p.reshape(n, -1), bf(w_ple_gate), bf(w_ple_proj), _tile(n, 512), _tile(d, 1024))
    return out.reshape(batch, seq, d)


def kernel(x, p, norm_ffn1, w1_gate, w1_up, w1_down, norm_mix, w_in, q_a_norm, w_q_b, kv_a_norm, w_kv_b, q_norm, k_norm, mu_shift, w0, w_w2, a0, w_a2, w_g2, k_k, k_a, r_k, ln_x_w, ln_x_b, w_out, norm_ffn2, w2_gate, w2_up, w2_down, norm_ple, w_ple_gate, w_ple_proj):
    depth = p.shape[0]
    for i in range(depth):
        x = _layer(x, p[i], norm_ffn1[i], w1_gate[i], w1_up[i], w1_down[i], norm_mix[i], w_in[i], q_a_norm[i],
                   w_q_b[i], kv_a_norm[i], w_kv_b[i], q_norm[i], k_norm[i], mu_shift[i], w0[i], w_w2[i], a0[i],
                   w_a2[i], w_g2[i], k_k[i], k_a[i], r_k[i], ln_x_w[i], ln_x_b[i], w_out[i], norm_ffn2[i],
                   w2_gate[i], w2_up[i], w2_down[i], norm_ple[i], w_ple_gate[i], w_ple_proj[i])
    return x
```

```python
import functools
import math

import jax
import jax.numpy as jnp
from jax import lax
from jax.experimental import pallas as pl
from jax.experimental.pallas import tpu as pltpu

F32 = jnp.float32
BF16 = jnp.bfloat16

EPS = 1e-6
GN_EPS = 64e-5
MLA_HEADS = 8
QK_NOPE = 128
QK_ROPE = 64
QK_HEAD = QK_NOPE + QK_ROPE
V_HEAD = 128
Q_LORA = 512
KV_LORA = 256
ROPE_BASE = 10000.0
RWKV_HEAD = 64
RWKV_HEADS = 16
RWKV_DIM = RWKV_HEADS * RWKV_HEAD
LANES = 128
CHUNK = 64
VMEM_LIMIT = 56 * 1024 * 1024
SCAN_PASSES = (1, 1, 1, 1, 1, 1)


def _cparams(sem):
    return pltpu.CompilerParams(dimension_semantics=sem, vmem_limit_bytes=VMEM_LIMIT)


def _dot(a, b):
    return jnp.dot(a, b, preferred_element_type=F32)


def _rms(x, g):
    return x * lax.rsqrt(jnp.mean(x * x, axis=-1, keepdims=True) + EPS) * g


def _ffn_kernel(x_ref, g_ref, wg_ref, wu_ref, wd_ref, o_ref, h_ref):
    j = pl.program_id(1)

    @pl.when(j == 0)
    def _():
        h_ref[...] = _rms(x_ref[...], g_ref[...]).astype(BF16)
        o_ref[...] = jnp.zeros_like(o_ref)

    h = h_ref[...]
    gate = _dot(h, wg_ref[...].astype(BF16))
    up = _dot(h, wu_ref[...].astype(BF16))
    act = (gate * jax.nn.sigmoid(gate) * up).astype(BF16)
    o_ref[...] += _dot(act, wd_ref[...].astype(BF16))

    @pl.when(j == pl.num_programs(1) - 1)
    def _():
        o_ref[...] = x_ref[...] + 0.5 * o_ref[...]


def _ffn(x, g, wg, wu, wd, tm, tf):
    n, d = x.shape
    dff = wg.shape[1]
    return pl.pallas_call(
        _ffn_kernel,
        grid=(n // tm, dff // tf),
        in_specs=[
            pl.BlockSpec((tm, d), lambda i, j: (i, 0)),
            pl.BlockSpec((1, d), lambda i, j: (0, 0)),
            pl.BlockSpec((d, tf), lambda i, j: (0, j)),
            pl.BlockSpec((d, tf), lambda i, j: (0, j)),
            pl.BlockSpec((tf, d), lambda i, j: (j, 0)),
        ],
        out_specs=pl.BlockSpec((tm, d), lambda i, j: (i, 0)),
        out_shape=jax.ShapeDtypeStruct((n, d), F32),
        scratch_shapes=[pltpu.VMEM((tm, d), BF16)],
        compiler_params=_cparams(("parallel", "arbitrary")),
        name="ffn",
    )(x, g, wg, wu, wd)


def _norm_mm_kernel(x_ref, g_ref, w_ref, o_ref, h_ref):
    @pl.when(pl.program_id(1) == 0)
    def _():
        h_ref[...] = _rms(x_ref[...], g_ref[...]).astype(BF16)

    o_ref[...] = _dot(h_ref[...], w_ref[...])


def _norm_mm(x, g, w, tm, tn):
    n, d = x.shape
    nc = w.shape[1]
    return pl.pallas_call(
        _norm_mm_kernel,
        grid=(n // tm, nc // tn),
        in_specs=[
            pl.BlockSpec((tm, d), lambda i, j: (i, 0)),
            pl.BlockSpec((1, d), lambda i, j: (0, 0)),
            pl.BlockSpec((d, tn), lambda i, j: (0, j)),
        ],
        out_specs=pl.BlockSpec((tm, tn), lambda i, j: (i, j)),
        out_shape=jax.ShapeDtypeStruct((n, nc), F32),
        scratch_shapes=[pltpu.VMEM((tm, d), BF16)],
        compiler_params=_cparams(("parallel", "arbitrary")),
        name="norm_mm",
    )(x, g, w)


def _seg2(x):
    lo = lax.broadcasted_iota(jnp.int32, x.shape, 1) < RWKV_HEAD
    s0 = jnp.sum(jnp.where(lo, x, 0.0), axis=-1, keepdims=True)
    s1 = jnp.sum(jnp.where(lo, 0.0, x), axis=-1, keepdims=True)
    return s0, s1


def _rope_pair(y, cos_t, sin_t):
    lane = lax.broadcasted_iota(jnp.int32, y.shape, 1)
    first = (lane % QK_ROPE) < (QK_ROPE // 2)
    rot = jnp.where(first, pltpu.roll(y, LANES - QK_ROPE // 2, axis=1), pltpu.roll(y, QK_ROPE // 2, axis=1))
    return y * cos_t + rot * sin_t


def _mla_prep_kernel(x_ref, gmix_ref, win_ref, qag_ref, wqb_ref, kvag_ref, wkvb_ref,
                     gqn_ref, gqp_ref, gkn_ref, gkp_ref, cos_ref, sin_ref,
                     q_ref, k_ref, v_ref):
    scale = 1.0 / math.sqrt(QK_HEAD)
    h = _rms(x_ref[...], gmix_ref[...]).astype(BF16)
    z = _dot(h, win_ref[...])
    q_lat = z[:, :Q_LORA]
    kv_lat = z[:, Q_LORA:Q_LORA + KV_LORA]
    kpe2 = z[:, Q_LORA + KV_LORA:]
    qf = _dot(_rms(q_lat, qag_ref[...]).astype(BF16), wqb_ref[...])
    kvf = _dot(_rms(kv_lat, kvag_ref[...]).astype(BF16), wkvb_ref[...])
    cos_t = cos_ref[...]
    sin_t = sin_ref[...]
    lo = lax.broadcasted_iota(jnp.int32, cos_t.shape, 1) < QK_ROPE
    kpe_ss, _ = _seg2(kpe2 * kpe2)
    nope_w = MLA_HEADS * QK_NOPE
    for j in range(MLA_HEADS // 2):
        qp = qf[:, nope_w + j * LANES: nope_w + (j + 1) * LANES]
        qs0, qs1 = _seg2(qp * qp)
        rs_q = []
        rs_k = []
        for e, qs in ((0, qs0), (1, qs1)):
            hd = 2 * j + e
            qn = qf[:, hd * QK_NOPE:(hd + 1) * QK_NOPE]
            rq = lax.rsqrt((jnp.sum(qn * qn, axis=-1, keepdims=True) + qs) * (1.0 / QK_HEAD) + EPS)
            q_ref[0, hd, :, :QK_NOPE] = (qn * rq * gqn_ref[...] * scale).astype(BF16)
            rs_q.append(rq)
            kn = kvf[:, hd * 2 * QK_NOPE: hd * 2 * QK_NOPE + QK_NOPE]
            rk = lax.rsqrt((jnp.sum(kn * kn, axis=-1, keepdims=True) + kpe_ss) * (1.0 / QK_HEAD) + EPS)
            k_ref[0, hd, :, :QK_NOPE] = (kn * rk * gkn_ref[...]).astype(BF16)
            rs_k.append(rk)
            v_ref[0, hd] = kvf[:, hd * 2 * QK_NOPE + QK_NOPE:(hd + 1) * 2 * QK_NOPE].astype(BF16)
        yq = _rope_pair(qp * jnp.where(lo, rs_q[0], rs_q[1]) * gqp_ref[...], cos_t, sin_t) * scale
        yk = _rope_pair(kpe2 * jnp.where(lo, rs_k[0], rs_k[1]) * gkp_ref[...], cos_t, sin_t)
        q_ref[0, 2 * j, :, QK_NOPE:] = jnp.where(lo, yq, 0.0).astype(BF16)
        q_ref[0, 2 * j + 1, :, QK_NOPE:] = jnp.where(lo, 0.0, yq).astype(BF16)
        k_ref[0, 2 * j, :, QK_NOPE:] = jnp.where(lo, yk, 0.0).astype(BF16)
        k_ref[0, 2 * j + 1, :, QK_NOPE:] = jnp.where(lo, 0.0, yk).astype(BF16)


def _mla_prep(x, gmix, win, qag, wqb, kvag, wkvb, gqn, gqp, gkn, gkp, cos_t, sin_t, batch, seq, tm):
    n, d = x.shape
    spt = seq // tm
    full = lambda a: pl.BlockSpec(a.shape, lambda i: (0,) * a.ndim)
    qk_w = 2 * LANES
    return pl.pallas_call(
        _mla_prep_kernel,
        grid=(n // tm,),
        in_specs=[
            pl.BlockSpec((tm, d), lambda i: (i, 0)),
            full(gmix), full(win), full(qag), full(wqb), full(kvag), full(wkvb),
            full(gqn), full(gqp), full(gkn), full(gkp),
            pl.BlockSpec((tm, LANES), lambda i: (i % spt, 0)),
            pl.BlockSpec((tm, LANES), lambda i: (i % spt, 0)),
        ],
        out_specs=[
            pl.BlockSpec((1, MLA_HEADS, tm, qk_w), lambda i: (i // spt, 0, i % spt, 0)),
            pl.BlockSpec((1, MLA_HEADS, tm, qk_w), lambda i: (i // spt, 0, i % spt, 0)),
            pl.BlockSpec((1, MLA_HEADS, tm, V_HEAD), lambda i: (i // spt, 0, i % spt, 0)),
        ],
        out_shape=[
            jax.ShapeDtypeStruct((batch, MLA_HEADS, seq, qk_w), BF16),
            jax.ShapeDtypeStruct((batch, MLA_HEADS, seq, qk_w), BF16),
            jax.ShapeDtypeStruct((batch, MLA_HEADS, seq, V_HEAD), BF16),
        ],
        compiler_params=_cparams(("parallel",)),
        name="mla_prep",
    )(x, gmix, win, qag, wqb, kvag, wkvb, gqn, gqp, gkn, gkp, cos_t, sin_t)


def _flash_kernel(q_ref, k_ref, v_ref, o_ref, m_ref, l_ref, acc_ref, *, tk, nsub):
    qi = pl.program_id(2)
    m_ref[...] = jnp.full_like(m_ref, -1e30)
    l_ref[...] = jnp.zeros_like(l_ref)
    acc_ref[...] = jnp.zeros_like(acc_ref)
    rep = tk // LANES
    causal = lax.broadcasted_iota(jnp.int32, (tk, tk), 1) <= lax.broadcasted_iota(jnp.int32, (tk, tk), 0)

    def step(sub, j, diag):
        rows = pl.ds(sub * tk, tk)
        kv = pl.ds(pl.multiple_of(j * tk, tk), tk)
        s = lax.dot_general(q_ref[0, 0, rows, :], k_ref[0, 0, kv, :], (((1,), (1,)), ((), ())),
                            preferred_element_type=F32)
        if diag:
            s = jnp.where(causal, s, -1e30)
        m_prev = m_ref[rows, :]
        m_new = jnp.maximum(m_prev, jnp.max(s, axis=-1, keepdims=True))
        alpha = jnp.exp(m_prev - m_new)
        p = jnp.exp(s - jnp.concatenate([m_new] * rep, axis=1))
        l_ref[rows, :] = alpha * l_ref[rows, :] + jnp.sum(p, axis=-1, keepdims=True)
        acc_ref[rows, :] = alpha * acc_ref[rows, :] + _dot(p.astype(BF16), v_ref[0, 0, kv, :])
        m_ref[rows, :] = m_new

    def body(j, carry):
        for sub in range(nsub):
            step(sub, j, False)
        return carry

    lax.fori_loop(0, qi * nsub, body, 0)
    for e in range(nsub):
        for sub in range(e, nsub):
            step(sub, qi * nsub + e, sub == e)
    o_ref[0] = (acc_ref[...] / l_ref[...]).astype(o_ref.dtype)


def _flash(q, k, v, tq, tk):
    b, hh, s, dk = q.shape
    dv = v.shape[-1]
    assert dv == LANES and tq % tk == 0
    return pl.pallas_call(
        functools.partial(_flash_kernel, tk=tk, nsub=tq // tk),
        grid=(b, hh, s // tq),
        in_specs=[
            pl.BlockSpec((1, 1, tq, dk), lambda bi, h, qi: (bi, h, qi, 0)),
            pl.BlockSpec((1, 1, s, dk), lambda bi, h, qi: (bi, h, 0, 0)),
            pl.BlockSpec((1, 1, s, dv), lambda bi, h, qi: (bi, h, 0, 0)),
        ],
        out_specs=pl.BlockSpec((1, tq, dv), lambda bi, h, qi: (bi, qi, h)),
        out_shape=jax.ShapeDtypeStruct((b, s, hh * dv), BF16),
        scratch_shapes=[pltpu.VMEM((tq, LANES), F32), pltpu.VMEM((tq, LANES), F32), pltpu.VMEM((tq, dv), F32)],
        compiler_params=_cparams(("parallel", "parallel", "arbitrary")),
        name="flash",
    )(q, k, v)


def _rwkv_prep_kernel(z_ref, zp_ref, mu_ref, w0_ref, ww_ref, a0_ref, wa_ref, wg_ref, kk_ref, ka_ref,
                      r_o, k_o, v_o, kk_o, a_o, lw_o, g_o, *, seq):
    tm = z_ref.shape[0]
    z = z_ref[...]
    first = (pl.program_id(0) * tm) % seq == 0
    prow = jnp.where(first, 0.0, zp_ref[7:8, :])
    rowid = lax.broadcasted_iota(jnp.int32, z.shape, 0)
    prev = jnp.where(rowid == 0, prow, pltpu.roll(z, 1, axis=0))
    zs = z + (prev - z) * mu_ref[...]
    d = RWKV_DIM
    r = zs[:, :d]
    k = zs[:, d:2 * d]
    v = zs[:, 2 * d:3 * d]
    wa_lo = zs[:, 3 * d:3 * d + LANES]
    g_lo = zs[:, 3 * d + LANES:]
    wpre = w0_ref[...] + _dot(jnp.tanh(wa_lo).astype(BF16), ww_ref[...])
    y = -wpre
    w = -(jnp.maximum(y, 0.0) + jnp.log(1.0 + jnp.exp(-jnp.abs(y)))) - 0.5
    a = jax.nn.sigmoid(a0_ref[...] + _dot(wa_lo.astype(BF16), wa_ref[...]))
    g = _dot(jax.nn.sigmoid(g_lo).astype(BF16), wg_ref[...])
    r_o[...] = r
    v_o[...] = v
    a_o[...] = a
    g_o[...] = g
    lw_o[...] = -jnp.exp(w)
    k_o[...] = k * (1.0 + (a - 1.0) * ka_ref[...])
    kk = k * kk_ref[...]
    lo = lax.broadcasted_iota(jnp.int32, (tm, LANES), 1) < RWKV_HEAD
    for t in range(d // LANES):
        kt = kk[:, t * LANES:(t + 1) * LANES]
        s0, s1 = _seg2(kt * kt)
        nrm = jnp.where(lo, jnp.maximum(jnp.sqrt(s0), 1e-12), jnp.maximum(jnp.sqrt(s1), 1e-12))
        kk_o[:, t * LANES:(t + 1) * LANES] = kt / nrm


def _rwkv_prep(z, mu, w0, ww, a0, wa, wg, kkp, kap, seq, tm):
    n, zin = z.shape
    d = RWKV_DIM
    full = lambda a: pl.BlockSpec(a.shape, lambda i: (0,) * a.ndim)
    out = pl.BlockSpec((tm, d), lambda i: (i, 0))
    return pl.pallas_call(
        functools.partial(_rwkv_prep_kernel, seq=seq),
        grid=(n // tm,),
        in_specs=[
            pl.BlockSpec((tm, zin), lambda i: (i, 0)),
            pl.BlockSpec((8, zin), lambda i: (jnp.maximum(i * (tm // 8) - 1, 0), 0)),
            full(mu), full(w0), full(ww), full(a0), full(wa), full(wg), full(kkp), full(kap),
        ],
        out_specs=[out] * 7,
        out_shape=[jax.ShapeDtypeStruct((n, d), F32)] * 7,
        compiler_params=_cparams(("parallel",)),
        name="rwkv_prep",
    )(z, z, mu, w0, ww, a0, wa, wg, kkp, kap)


def _split(x):
    hi = x.astype(BF16)
    return hi, (x - hi.astype(F32)).astype(BF16)


def _bmm(a, b, spec, passes):
    mm = lambda u, w: jnp.einsum(spec, u, w, preferred_element_type=F32)
    if passes == 1:
        return mm(a.astype(BF16), b.astype(BF16))
    ah, al = _split(a)
    bh, bl = _split(b)
    return mm(ah, bh) + mm(al, bh) + mm(ah, bl)


def _tiles(x):
    return jnp.stack([x[:, t * LANES:(t + 1) * LANES] for t in range(x.shape[1] // LANES)])


def _rwkv_scan_kernel(r_ref, k_ref, v_ref, kk_ref, a_ref, lw_ref, g_ref, rk_ref, lnw_ref, lnb_ref,
                      o_ref, h_ref, *, passes):
    c = CHUNK
    c2 = 2 * c

    @pl.when(pl.program_id(1) == 0)
    def _():
        h_ref[...] = jnp.zeros_like(h_ref)

    lw = lw_ref[...]
    ti = lax.broadcasted_iota(jnp.int32, (c, c), 0)
    si = lax.broadcasted_iota(jnp.int32, (c, c), 1)
    tril = jnp.where(si <= ti, 1.0, 0.0)
    lwh, lwl = _split(lw)
    lwl2 = (lw - lwh.astype(F32) - lwl.astype(F32)).astype(BF16)
    trilb = tril.astype(BF16)
    cum = _dot(trilb, lwh) + _dot(trilb, lwl) + _dot(trilb, lwl2)
    p = jnp.exp(cum)
    pinv = jnp.exp(-cum)
    pprev = jnp.exp(cum - lw)
    pend = jnp.exp(cum[c - 1:c, :] - cum)
    r = r_ref[...]
    k = k_ref[...]
    v = v_ref[...]
    kk = kk_ref[...]
    b = kk * a_ref[...]
    at = _tiles(-kk * pprev)
    bt = _tiles(b * pinv)
    kt = _tiles(k * pinv)
    rt = _tiles(r * p)
    bh = _tiles(b * pend)
    kh = _tiles(k * pend)
    vt = _tiles(v)
    pc = _tiles(jnp.exp(cum[c - 1:c, :]))

    lo = lax.broadcasted_iota(jnp.int32, (1, 1, LANES), 2) < RWKV_HEAD

    def bd(x):
        return jnp.concatenate([jnp.where(lo, x, 0.0), jnp.where(lo, 0.0, x)], axis=1)

    a_bd, b_bd, k_bd, r_bd, v_bd = bd(at), bd(bt), bd(kt), bd(rt), bd(vt)
    bh_bd, kh_bd = bd(bh), bd(kh)

    amat = _bmm(jnp.concatenate([a_bd, r_bd], axis=1), jnp.concatenate([b_bd, k_bd], axis=1),
                "pil,pjl->pij", passes[0])
    ri = lax.broadcasted_iota(jnp.int32, (1, c2, c2), 1)
    ci = lax.broadcasted_iota(jnp.int32, (1, c2, c2), 2)
    same = (ri // c) == (ci // c)
    strict = same & ((ri % c) > (ci % c))
    incl = same & ((ri % c) >= (ci % c))
    a_ab = jnp.where(strict, amat[:, :c2, :c2], 0.0)
    a_ak = jnp.where(strict, amat[:, :c2, c2:], 0.0)
    a_rb = jnp.where(incl, amat[:, c2:, :c2], 0.0)
    a_rk = jnp.where(incl, amat[:, c2:, c2:], 0.0)

    eye = jnp.where(ri == ci, 1.0, 0.0)
    tinv = eye + a_ab
    apow = a_ab
    for _ in range(int(math.log2(c)) - 1):
        apow = _bmm(apow, apow, "pij,pjk->pik", passes[1])
        tinv = tinv + _bmm(tinv, apow, "pij,pjk->pik", passes[1])

    akv = _bmm(a_ak, v_bd, "pij,pjk->pik", passes[2])
    wu = _bmm(tinv, jnp.concatenate([a_bd, akv], axis=2), "pij,pjk->pik", passes[3])
    rhs = jnp.concatenate([wu, jnp.concatenate([jnp.zeros_like(v_bd), v_bd], axis=2)], axis=1)
    top = _bmm(jnp.concatenate([a_rb, a_rk], axis=2), rhs, "pij,pjk->pik", passes[4])
    bot = _bmm(jnp.concatenate([bh_bd, kh_bd], axis=1), rhs, "psk,psn->pkn", passes[4])
    q_eff = r_bd + top[:, :, :c2]
    o_intra = top[:, :, c2:]
    m_mat = eye * pc + bot[:, :, :c2]
    g_mat = bot[:, :, c2:]

    h0 = h_ref[...]
    seq_out = _bmm(jnp.concatenate([q_eff, m_mat], axis=1), h0, "pij,pjk->pik", passes[5])
    o_bd = seq_out[:, :c2, :] + o_intra
    h_ref[...] = seq_out[:, c2:, :] + g_mat
    o = jnp.where(lo, o_bd[:, :c, :], o_bd[:, c:, :])

    rk = rk_ref[...]
    lnw = lnw_ref[...]
    lnb = lnb_ref[...]
    g = g_ref[...]
    lo2 = lax.broadcasted_iota(jnp.int32, (c, LANES), 1) < RWKV_HEAD
    inv_n = 1.0 / RWKV_HEAD
    for t in range(RWKV_DIM // LANES):
        sl = slice(t * LANES, (t + 1) * LANES)
        ot = o[t]
        s0, s1 = _seg2(ot)
        cen = ot - jnp.where(lo2, s0, s1) * inv_n
        q0, q1 = _seg2(cen * cen)
        on = cen * lax.rsqrt(jnp.where(lo2, q0, q1) * inv_n + GN_EPS) * lnw[:, sl] + lnb[:, sl]
        b0, b1 = _seg2(r[:, sl] * k[:, sl] * rk[:, sl])
        on = on + jnp.where(lo2, b0, b1) * v[:, sl]
        o_ref[:, sl] = (on * g[:, sl]).astype(o_ref.dtype)


def _rwkv_scan(r, k, v, kk, a, lw, g, rk, lnw, lnb, batch, seq, passes):
    n, d = r.shape
    nck = seq // CHUNK
    blk = pl.BlockSpec((CHUNK, d), lambda bi, ci: (bi * nck + ci, 0))
    full = lambda x: pl.BlockSpec(x.shape, lambda bi, ci: (0,) * x.ndim)
    return pl.pallas_call(
        functools.partial(_rwkv_scan_kernel, passes=passes),
        grid=(batch, nck),
        in_specs=[blk] * 7 + [full(rk), full(lnw), full(lnb)],
        out_specs=blk,
        out_shape=jax.ShapeDtypeStruct((n, d), BF16),
        scratch_shapes=[pltpu.VMEM((d // LANES, LANES, LANES), F32)],
        compiler_params=_cparams(("parallel", "arbitrary")),
        name="rwkv_scan",
    )(r, k, v, kk, a, lw, g, rk, lnw, lnb)


def _out_proj_kernel(x_ref, oa_ref, ob_ref, wa_ref, wb_ref, o_ref):
    o_ref[...] = x_ref[...] + _dot(oa_ref[...], wa_ref[...]) + _dot(ob_ref[...], wb_ref[...])


def _out_proj(x, oa, ob, wa, wb, tm, tn):
    n, d = x.shape
    ka = oa.shape[1]
    kb = ob.shape[1]
    return pl.pallas_call(
        _out_proj_kernel,
        grid=(n // tm, d // tn),
        in_specs=[
            pl.BlockSpec((tm, tn), lambda i, j: (i, j)),
            pl.BlockSpec((tm, ka), lambda i, j: (i, 0)),
            pl.BlockSpec((tm, kb), lambda i, j: (i, 0)),
            pl.BlockSpec((ka, tn), lambda i, j: (0, j)),
            pl.BlockSpec((kb, tn), lambda i, j: (0, j)),
        ],
        out_specs=pl.BlockSpec((tm, tn), lambda i, j: (i, j)),
        out_shape=jax.ShapeDtypeStruct((n, d), F32),
        compiler_params=_cparams(("parallel", "parallel")),
        name="out_proj",
    )(x, oa, ob, wa, wb)


def _ple_kernel(x_ref, g_ref, p_ref, wg_ref, wp_ref, o_ref, h_ref, *, tn):
    j = pl.program_id(1)

    @pl.when(j == 0)
    def _():
        h_ref[...] = _rms(x_ref[...], g_ref[...]).astype(BF16)

    gate = jax.nn.sigmoid(_dot(h_ref[...], wg_ref[...]))
    emb = _dot(p_ref[...].astype(BF16), wp_ref[...])
    col = pl.multiple_of(j * tn, LANES)
    o_ref[...] = x_ref[:, pl.ds(col, tn)] + gate * emb


def _ple(x, g, p, wg, wp, tm, tn):
    n, d = x.shape
    pd = p.shape[1]
    return pl.pallas_call(
        functools.partial(_ple_kernel, tn=tn),
        grid=(n // tm, d // tn),
        in_specs=[
            pl.BlockSpec((tm, d), lambda i, j: (i, 0)),
            pl.BlockSpec((1, d), lambda i, j: (0, 0)),
            pl.BlockSpec((tm, pd), lambda i, j: (i, 0)),
            pl.BlockSpec((d, tn), lambda i, j: (0, j)),
            pl.BlockSpec((pd, tn), lambda i, j: (0, j)),
        ],
        out_specs=pl.BlockSpec((tm, tn), lambda i, j: (i, j)),
        out_shape=jax.ShapeDtypeStruct((n, d), F32),
        scratch_shapes=[pltpu.VMEM((tm, d), BF16)],
        compiler_params=_cparams(("parallel", "arbitrary")),
        name="ple",
    )(x, g, p, wg, wp)


def _tile(n, pref):
    t = min(n, pref)
    assert n % t == 0, (n, t)
    return t


def _layer(x, p, norm_ffn1, w1_gate, w1_up, w1_down, norm_mix, w_in, q_a_norm, w_q_b, kv_a_norm, w_kv_b,
           q_norm, k_norm, mu_shift, w0, w_w2, a0, w_a2, w_g2, k_k, k_a, r_k, ln_x_w, ln_x_b, w_out,
           norm_ffn2, w2_gate, w2_up, w2_down, norm_ple, w_ple_gate, w_ple_proj):
    batch, seq, d = x.shape
    n = batch * seq
    row = lambda a: a.reshape(1, -1)
    bf = lambda a: a.astype(BF16)
    mla_in = Q_LORA + KV_LORA + QK_ROPE
    dff = w1_gate.shape[1]
    tf = _tile(dff, 256)
    tm_ffn = _tile(n, 1024)

    xf = x.reshape(n, d)
    x1 = _ffn(xf, row(norm_ffn1), w1_gate, w1_up, w1_down, tm_ffn, tf)

    w_mla = bf(jnp.concatenate([w_in[:, :mla_in], w_in[:, mla_in - QK_ROPE:mla_in]], axis=1))
    wq = w_q_b.reshape(Q_LORA, MLA_HEADS, QK_HEAD)
    wqb = bf(jnp.concatenate([wq[:, :, :QK_NOPE].reshape(Q_LORA, -1), wq[:, :, QK_NOPE:].reshape(Q_LORA, -1)], axis=1))
    inv_freq = 1.0 / (ROPE_BASE ** (jnp.arange(0, QK_ROPE, 2, dtype=F32) / QK_ROPE))
    ang = jnp.arange(seq, dtype=F32)[:, None] * inv_freq[None, :]
    cos, sin = jnp.cos(ang), jnp.sin(ang)
    cos_t = jnp.concatenate([cos, cos, cos, cos], axis=1)
    sin_t = jnp.concatenate([-sin, sin, -sin, sin], axis=1)
    pair = lambda a: row(jnp.concatenate([a, a]))
    tm_prep = _tile(seq, 256)
    q, k, v = _mla_prep(x1, row(norm_mix), w_mla, row(q_a_norm), wqb, row(kv_a_norm), bf(w_kv_b),
                        row(q_norm[:QK_NOPE]), pair(q_norm[QK_NOPE:]), row(k_norm[:QK_NOPE]),
                        pair(k_norm[QK_NOPE:]), cos_t, sin_t, batch, seq, tm_prep)
    o_mla = _flash(q, k, v, _tile(seq, 1024), _tile(seq, 512)).reshape(n, MLA_HEADS * V_HEAD)

    z = _norm_mm(x1, row(norm_mix), bf(w_in[:, mla_in:]), _tile(n, 512), 13 * LANES)
    zero = jnp.zeros_like(w_w2)
    ww = bf(jnp.concatenate([w_w2, zero], axis=0))
    wa = bf(jnp.concatenate([zero, w_a2], axis=0))
    rw = _rwkv_prep(z, row(mu_shift), row(w0), ww, row(a0), wa, bf(w_g2), row(k_k), row(k_a), seq, tm_prep)
    o_rwkv = _rwkv_scan(*rw, row(r_k), row(ln_x_w), row(ln_x_b), batch, seq, SCAN_PASSES)

    half = MLA_HEADS * V_HEAD
    x2 = _out_proj(x1, o_mla, o_rwkv, bf(w_out[:half]), bf(w_out[half:]), _tile(n, 512), _tile(d, 1024))
    x3 = _ffn(x2, row(norm_ffn2), w2_gate, w2_up, w2_down, tm_ffn, tf)
    out = _ple(x3, row(norm_ple), p.reshape(n, -1), bf(w_ple_gate), bf(w_ple_proj), _tile(n, 512), _tile(d, 1024))
    return out.reshape(batch, seq, d)


def kernel(x, p, norm_ffn1, w1_gate, w1_up, w1_down, norm_mix, w_in, q_a_norm, w_q_b, kv_a_norm, w_kv_b, q_norm, k_norm, mu_shift, w0, w_w2, a0, w_a2, w_g2, k_k, k_a, r_k, ln_x_w, ln_x_b, w_out, norm_ffn2, w2_gate, w2_up, w2_down, norm_ple, w_ple_gate, w_ple_proj):
    depth = p.shape[0]
    for i in range(depth):
        x = _layer(x, p[i], norm_ffn1[i], w1_gate[i], w1_up[i], w1_down[i], norm_mix[i], w_in[i], q_a_norm[i],
                   w_q_b[i], kv_a_norm[i], w_kv_b[i], q_norm[i], k_norm[i], mu_shift[i], w0[i], w_w2[i], a0[i],
                   w_a2[i], w_g2[i], k_k[i], k_a[i], r_k[i], ln_x_w[i], ln_x_b[i], w_out[i], norm_ffn2[i],
                   w2_gate[i], w2_up[i], w2_down[i], norm_ple[i], w_ple_gate[i], w_ple_proj[i])
    return x
```

```python
import functools
import math

import jax
import jax.numpy as jnp
from jax import lax
from jax.experimental import pallas as pl
from jax.experimental.pallas import tpu as pltpu

F32 = jnp.float32
BF16 = jnp.bfloat16

EPS = 1e-6
GN_EPS = 64e-5
MLA_HEADS = 8
QK_NOPE = 128
QK_ROPE = 64
QK_HEAD = QK_NOPE + QK_ROPE
V_HEAD = 128
Q_LORA = 512
KV_LORA = 256
ROPE_BASE = 10000.0
RWKV_HEAD = 64
RWKV_HEADS = 16
RWKV_DIM = RWKV_HEADS * RWKV_HEAD
LANES = 128
CHUNK = 64
VMEM_LIMIT = 56 * 1024 * 1024
SCAN_PASSES = (1, 1, 1, 1, 1, 1)


def _cparams(sem):
    return pltpu.CompilerParams(dimension_semantics=sem, vmem_limit_bytes=VMEM_LIMIT)


def _dot(a, b):
    return jnp.dot(a, b, preferred_element_type=F32)


def _rms(x, g):
    return x * lax.rsqrt(jnp.mean(x * x, axis=-1, keepdims=True) + EPS) * g


def _ffn_kernel(x_ref, g_ref, wg_ref, wu_ref, wd_ref, o_ref, h_ref):
    j = pl.program_id(1)

    @pl.when(j == 0)
    def _():
        h_ref[...] = _rms(x_ref[...], g_ref[...]).astype(BF16)
        o_ref[...] = jnp.zeros_like(o_ref)

    h = h_ref[...]
    gate = _dot(h, wg_ref[...].astype(BF16))
    up = _dot(h, wu_ref[...].astype(BF16))
    act = (gate * jax.nn.sigmoid(gate) * up).astype(BF16)
    o_ref[...] += _dot(act, wd_ref[...].astype(BF16))

    @pl.when(j == pl.num_programs(1) - 1)
    def _():
        o_ref[...] = x_ref[...] + 0.5 * o_ref[...]


def _ffn(x, g, wg, wu, wd, tm, tf):
    n, d = x.shape
    dff = wg.shape[1]
    return pl.pallas_call(
        _ffn_kernel,
        grid=(n // tm, dff // tf),
        in_specs=[
            pl.BlockSpec((tm, d), lambda i, j: (i, 0)),
            pl.BlockSpec((1, d), lambda i, j: (0, 0)),
            pl.BlockSpec((d, tf), lambda i, j: (0, j)),
            pl.BlockSpec((d, tf), lambda i, j: (0, j)),
            pl.BlockSpec((tf, d), lambda i, j: (j, 0)),
        ],
        out_specs=pl.BlockSpec((tm, d), lambda i, j: (i, 0)),
        out_shape=jax.ShapeDtypeStruct((n, d), F32),
        scratch_shapes=[pltpu.VMEM((tm, d), BF16)],
        compiler_params=_cparams(("parallel", "arbitrary")),
        name="ffn",
    )(x, g, wg, wu, wd)


def _norm_mm_kernel(x_ref, g_ref, w_ref, o_ref, h_ref):
    @pl.when(pl.program_id(1) == 0)
    def _():
        h_ref[...] = _rms(x_ref[...], g_ref[...]).astype(BF16)

    o_ref[...] = _dot(h_ref[...], w_ref[...])


def _norm_mm(x, g, w, tm, tn):
    n, d = x.shape
    nc = w.shape[1]
    return pl.pallas_call(
        _norm_mm_kernel,
        grid=(n // tm, nc // tn),
        in_specs=[
            pl.BlockSpec((tm, d), lambda i, j: (i, 0)),
            pl.BlockSpec((1, d), lambda i, j: (0, 0)),
            pl.BlockSpec((d, tn), lambda i, j: (0, j)),
        ],
        out_specs=pl.BlockSpec((tm, tn), lambda i, j: (i, j)),
        out_shape=jax.ShapeDtypeStruct((n, nc), F32),
        scratch_shapes=[pltpu.VMEM((tm, d), BF16)],
        compiler_params=_cparams(("parallel", "arbitrary")),
        name="norm_mm",
    )(x, g, w)


def _seg2(x):
    lo = lax.broadcasted_iota(jnp.int32, x.shape, 1) < RWKV_HEAD
    s0 = jnp.sum(jnp.where(lo, x, 0.0), axis=-1, keepdims=True)
    s1 = jnp.sum(jnp.where(lo, 0.0, x), axis=-1, keepdims=True)
    return s0, s1


def _rope_pair(y, cos_t, sin_t):
    lane = lax.broadcasted_iota(jnp.int32, y.shape, 1)
    first = (lane % QK_ROPE) < (QK_ROPE // 2)
    rot = jnp.where(first, pltpu.roll(y, LANES - QK_ROPE // 2, axis=1), pltpu.roll(y, QK_ROPE // 2, axis=1))
    return y * cos_t + rot * sin_t


def _mla_prep_kernel(x_ref, gmix_ref, win_ref, qag_ref, wqb_ref, kvag_ref, wkvb_ref,
                     gqn_ref, gqp_ref, gkn_ref, gkp_ref, cos_ref, sin_ref,
                     q_ref, k_ref, v_ref):
    scale = 1.0 / math.sqrt(QK_HEAD)
    h = _rms(x_ref[...], gmix_ref[...]).astype(BF16)
    z = _dot(h, win_ref[...])
    q_lat = z[:, :Q_LORA]
    kv_lat = z[:, Q_LORA:Q_LORA + KV_LORA]
    kpe2 = z[:, Q_LORA + KV_LORA:]
    qf = _dot(_rms(q_lat, qag_ref[...]).astype(BF16), wqb_ref[...])
    kvf = _dot(_rms(kv_lat, kvag_ref[...]).astype(BF16), wkvb_ref[...])
    cos_t = cos_ref[...]
    sin_t = sin_ref[...]
    lo = lax.broadcasted_iota(jnp.int32, cos_t.shape, 1) < QK_ROPE
    kpe_ss, _ = _seg2(kpe2 * kpe2)
    nope_w = MLA_HEADS * QK_NOPE
    for j in range(MLA_HEADS // 2):
        qp = qf[:, nope_w + j * LANES: nope_w + (j + 1) * LANES]
        qs0, qs1 = _seg2(qp * qp)
        rs_q = []
        rs_k = []
        for e, qs in ((0, qs0), (1, qs1)):
            hd = 2 * j + e
            qn = qf[:, hd * QK_NOPE:(hd + 1) * QK_NOPE]
            rq = lax.rsqrt((jnp.sum(qn * qn, axis=-1, keepdims=True) + qs) * (1.0 / QK_HEAD) + EPS)
            q_ref[0, hd, :, :QK_NOPE] = (qn * rq * gqn_ref[...] * scale).astype(BF16)
            rs_q.append(rq)
            kn = kvf[:, hd * 2 * QK_NOPE: hd * 2 * QK_NOPE + QK_NOPE]
            rk = lax.rsqrt((jnp.sum(kn * kn, axis=-1, keepdims=True) + kpe_ss) * (1.0 / QK_HEAD) + EPS)
            k_ref[0, hd, :, :QK_NOPE] = (kn * rk * gkn_ref[...]).astype(BF16)
            rs_k.append(rk)
            v_ref[0, hd] = kvf[:, hd * 2 * QK_NOPE + QK_NOPE:(hd + 1) * 2 * QK_NOPE].astype(BF16)
        yq = _rope_pair(qp * jnp.where(lo, rs_q[0], rs_q[1]) * gqp_ref[...], cos_t, sin_t) * scale
        yk = _rope_pair(kpe2 * jnp.where(lo, rs_k[0], rs_k[1]) * gkp_ref[...], cos_t, sin_t)
        q_ref[0, 2 * j, :, QK_NOPE:] = jnp.where(lo, yq, 0.0).astype(BF16)
        q_ref[0, 2 * j + 1, :, QK_NOPE:] = jnp.where(lo, 0.0, yq).astype(BF16)
        k_ref[0, 2 * j, :, QK_NOPE:] = jnp.where(lo, yk, 0.0).astype(BF16)
        k_ref[0, 2 * j + 1, :, QK_NOPE:] = jnp.where(lo, 0.0, yk).astype(BF16)


def _mla_prep(x, gmix, win, qag, wqb, kvag, wkvb, gqn, gqp, gkn, gkp, cos_t, sin_t, batch, seq, tm):
    n, d = x.shape
    spt = seq // tm
    full = lambda a: pl.BlockSpec(a.shape, lambda i: (0,) * a.ndim)
    qk_w = 2 * LANES
    return pl.pallas_call(
        _mla_prep_kernel,
        grid=(n // tm,),
        in_specs=[
            pl.BlockSpec((tm, d), lambda i: (i, 0)),
            full(gmix), full(win), full(qag), full(wqb), full(kvag), full(wkvb),
            full(gqn), full(gqp), full(gkn), full(gkp),
            pl.BlockSpec((tm, LANES), lambda i: (i % spt, 0)),
            pl.BlockSpec((tm, LANES), lambda i: (i % spt, 0)),
        ],
        out_specs=[
            pl.BlockSpec((1, MLA_HEADS, tm, qk_w), lambda i: (i // spt, 0, i % spt, 0)),
            pl.BlockSpec((1, MLA_HEADS, tm, qk_w), lambda i: (i // spt, 0, i % spt, 0)),
            pl.BlockSpec((1, MLA_HEADS, tm, V_HEAD), lambda i: (i // spt, 0, i % spt, 0)),
        ],
        out_shape=[
            jax.ShapeDtypeStruct((batch, MLA_HEADS, seq, qk_w), BF16),
            jax.ShapeDtypeStruct((batch, MLA_HEADS, seq, qk_w), BF16),
            jax.ShapeDtypeStruct((batch, MLA_HEADS, seq, V_HEAD), BF16),
        ],
        compiler_params=_cparams(("parallel",)),
        name="mla_prep",
    )(x, gmix, win, qag, wqb, kvag, wkvb, gqn, gqp, gkn, gkp, cos_t, sin_t)


def _flash_kernel(q_ref, k_ref, v_ref, o_ref, m_ref, l_ref, acc_ref, *, tk, nsub):
    qi = pl.program_id(2)
    m_ref[...] = jnp.full_like(m_ref, -1e30)
    l_ref[...] = jnp.zeros_like(l_ref)
    acc_ref[...] = jnp.zeros_like(acc_ref)
    rep = tk // LANES
    causal = lax.broadcasted_iota(jnp.int32, (tk, tk), 1) <= lax.broadcasted_iota(jnp.int32, (tk, tk), 0)

    def step(sub, j, diag):
        rows = pl.ds(sub * tk, tk)
        kv = pl.ds(pl.multiple_of(j * tk, tk), tk)
        s = lax.dot_general(q_ref[0, 0, rows, :], k_ref[0, 0, kv, :], (((1,), (1,)), ((), ())),
                            preferred_element_type=F32)
        if diag:
            s = jnp.where(causal, s, -1e30)
        m_prev = m_ref[rows, :]
        m_new = jnp.maximum(m_prev, jnp.max(s, axis=-1, keepdims=True))
        alpha = jnp.exp(m_prev - m_new)
        p = jnp.exp(s - jnp.concatenate([m_new] * rep, axis=1))
        l_ref[rows, :] = alpha * l_ref[rows, :] + jnp.sum(p, axis=-1, keepdims=True)
        acc_ref[rows, :] = alpha * acc_ref[rows, :] + _dot(p.astype(BF16), v_ref[0, 0, kv, :])
        m_ref[rows, :] = m_new

    def body(j, carry):
        for sub in range(nsub):
            step(sub, j, False)
        return carry

    lax.fori_loop(0, qi * nsub, body, 0)
    for e in range(nsub):
        for sub in range(e, nsub):
            step(sub, qi * nsub + e, sub == e)
    o_ref[0] = (acc_ref[...] / l_ref[...]).astype(o_ref.dtype)


def _flash(q, k, v, tq, tk):
    b, hh, s, dk = q.shape
    dv = v.shape[-1]
    assert dv == LANES and tq % tk == 0
    return pl.pallas_call(
        functools.partial(_flash_kernel, tk=tk, nsub=tq // tk),
        grid=(b, hh, s // tq),
        in_specs=[
            pl.BlockSpec((1, 1, tq, dk), lambda bi, h, qi: (bi, h, qi, 0)),
            pl.BlockSpec((1, 1, s, dk), lambda bi, h, qi: (bi, h, 0, 0)),
            pl.BlockSpec((1, 1, s, dv), lambda bi, h, qi: (bi, h, 0, 0)),
        ],
        out_specs=pl.BlockSpec((1, tq, dv), lambda bi, h, qi: (bi, qi, h)),
        out_shape=jax.ShapeDtypeStruct((b, s, hh * dv), BF16),
        scratch_shapes=[pltpu.VMEM((tq, LANES), F32), pltpu.VMEM((tq, LANES), F32), pltpu.VMEM((tq, dv), F32)],
        compiler_params=_cparams(("parallel", "parallel", "arbitrary")),
        name="flash",
    )(q, k, v)


def _rwkv_prep_kernel(z_ref, zp_ref, mu_ref, w0_ref, ww_ref, a0_ref, wa_ref, wg_ref, kk_ref, ka_ref,
                      r_o, k_o, v_o, kk_o, a_o, lw_o, g_o, *, seq):
    tm = z_ref.shape[0]
    z = z_ref[...]
    first = (pl.program_id(0) * tm) % seq == 0
    prow = jnp.where(first, 0.0, zp_ref[7:8, :])
    rowid = lax.broadcasted_iota(jnp.int32, z.shape, 0)
    prev = jnp.where(rowid == 0, prow, pltpu.roll(z, 1, axis=0))
    zs = z + (prev - z) * mu_ref[...]
    d = RWKV_DIM
    r = zs[:, :d]
    k = zs[:, d:2 * d]
    v = zs[:, 2 * d:3 * d]
    wa_lo = zs[:, 3 * d:3 * d + LANES]
    g_lo = zs[:, 3 * d + LANES:]
    wpre = w0_ref[...] + _dot(jnp.tanh(wa_lo).astype(BF16), ww_ref[...])
    y = -wpre
    w = -(jnp.maximum(y, 0.0) + jnp.log(1.0 + jnp.exp(-jnp.abs(y)))) - 0.5
    a = jax.nn.sigmoid(a0_ref[...] + _dot(wa_lo.astype(BF16), wa_ref[...]))
    g = _dot(jax.nn.sigmoid(g_lo).astype(BF16), wg_ref[...])
    r_o[...] = r
    v_o[...] = v
    a_o[...] = a
    g_o[...] = g
    lw_o[...] = -jnp.exp(w)
    k_o[...] = k * (1.0 + (a - 1.0) * ka_ref[...])
    kk = k * kk_ref[...]
    lo = lax.broadcasted_iota(jnp.int32, (tm, LANES), 1) < RWKV_HEAD
    for t in range(d // LANES):
        kt = kk[:, t * LANES:(t + 1) * LANES]
        s0, s1 = _seg2(kt * kt)
        nrm = jnp.where(lo, jnp.maximum(jnp.sqrt(s0), 1e-12), jnp.maximum(jnp.sqrt(s1), 1e-12))
        kk_o[:, t * LANES:(t + 1) * LANES] = kt / nrm


def _rwkv_prep(z, mu, w0, ww, a0, wa, wg, kkp, kap, seq, tm):
    n, zin = z.shape
    d = RWKV_DIM
    full = lambda a: pl.BlockSpec(a.shape, lambda i: (0,) * a.ndim)
    out = pl.BlockSpec((tm, d), lambda i: (i, 0))
    return pl.pallas_call(
        functools.partial(_rwkv_prep_kernel, seq=seq),
        grid=(n // tm,),
        in_specs=[
            pl.BlockSpec((tm, zin), lambda i: (i, 0)),
            pl.BlockSpec((8, zin), lambda i: (jnp.maximum(i * (tm // 8) - 1, 0), 0)),
            full(mu), full(w0), full(ww), full(a0), full(wa), full(wg), full(kkp), full(kap),
        ],
        out_specs=[out] * 7,
        out_shape=[jax.ShapeDtypeStruct((n, d), F32)] * 7,
        compiler_params=_cparams(("parallel",)),
        name="rwkv_prep",
    )(z, z, mu, w0, ww, a0, wa, wg, kkp, kap)


def _split(x):
    hi = x.astype(BF16)
    return hi, (x - hi.astype(F32)).astype(BF16)


def _bmm(a, b, spec, passes):
    mm = lambda u, w: jnp.einsum(spec, u, w, preferred_element_type=F32)
    if passes == 1:
        return mm(a.astype(BF16), b.astype(BF16))
    ah, al = _split(a)
    bh, bl = _split(b)
    return mm(ah, bh) + mm(al, bh) + mm(ah, bl)


def _tiles(x):
    return jnp.stack([x[:, t * LANES:(t + 1) * LANES] for t in range(x.shape[1] // LANES)])


def _rwkv_scan_kernel(r_ref, k_ref, v_ref, kk_ref, a_ref, lw_ref, g_ref, rk_ref, lnw_ref, lnb_ref,
                      o_ref, h_ref, *, passes):
    c = CHUNK
    c2 = 2 * c

    @pl.when(pl.program_id(1) == 0)
    def _():
        h_ref[...] = jnp.zeros_like(h_ref)

    lw = lw_ref[...]
    ti = lax.broadcasted_iota(jnp.int32, (c, c), 0)
    si = lax.broadcasted_iota(jnp.int32, (c, c), 1)
    tril = jnp.where(si <= ti, 1.0, 0.0)
    lwh, lwl = _split(lw)
    lwl2 = (lw - lwh.astype(F32) - lwl.astype(F32)).astype(BF16)
    trilb = tril.astype(BF16)
    cum = _dot(trilb, lwh) + _dot(trilb, lwl) + _dot(trilb, lwl2)
    p = jnp.exp(cum)
    pinv = jnp.exp(-cum)
    pprev = jnp.exp(cum - lw)
    pend = jnp.exp(cum[c - 1:c, :] - cum)
    r = r_ref[...]
    k = k_ref[...]
    v = v_ref[...]
    kk = kk_ref[...]
    b = kk * a_ref[...]
    at = _tiles(-kk * pprev)
    bt = _tiles(b * pinv)
    kt = _tiles(k * pinv)
    rt = _tiles(r * p)
    bh = _tiles(b * pend)
    kh = _tiles(k * pend)
    vt = _tiles(v)
    pc = _tiles(jnp.exp(cum[c - 1:c, :]))

    lo = lax.broadcasted_iota(jnp.int32, (1, 1, LANES), 2) < RWKV_HEAD

    def bd(x):
        return jnp.concatenate([jnp.where(lo, x, 0.0), jnp.where(lo, 0.0, x)], axis=1)

    a_bd, b_bd, k_bd, r_bd, v_bd = bd(at), bd(bt), bd(kt), bd(rt), bd(vt)
    bh_bd, kh_bd = bd(bh), bd(kh)

    amat = _bmm(jnp.concatenate([a_bd, r_bd], axis=1), jnp.concatenate([b_bd, k_bd], axis=1),
                "pil,pjl->pij", passes[0])
    ri = lax.broadcasted_iota(jnp.int32, (1, c2, c2), 1)
    ci = lax.broadcasted_iota(jnp.int32, (1, c2, c2), 2)
    same = (ri // c) == (ci // c)
    strict = same & ((ri % c) > (ci % c))
    incl = same & ((ri % c) >= (ci % c))
    a_ab = jnp.where(strict, amat[:, :c2, :c2], 0.0)
    a_ak = jnp.where(strict, amat[:, :c2, c2:], 0.0)
    a_rb = jnp.where(incl, amat[:, c2:, :c2], 0.0)
    a_rk = jnp.where(incl, amat[:, c2:, c2:], 0.0)

    eye = jnp.where(ri == ci, 1.0, 0.0)
    tinv = eye + a_ab
    apow = a_ab
    for _ in range(int(math.log2(c)) - 1):
        apow = _bmm(apow, apow, "pij,pjk->pik", passes[1])
        tinv = tinv + _bmm(tinv, apow, "pij,pjk->pik", passes[1])

    akv = _bmm(a_ak, v_bd, "pij,pjk->pik", passes[2])
    wu = _bmm(tinv, jnp.concatenate([a_bd, akv], axis=2), "pij,pjk->pik", passes[3])
    rhs = jnp.concatenate([wu, jnp.concatenate([jnp.zeros_like(v_bd), v_bd], axis=2)], axis=1)
    top = _bmm(jnp.concatenate([a_rb, a_rk], axis=2), rhs, "pij,pjk->pik", passes[4])
    bot = _bmm(jnp.concatenate([bh_bd, kh_bd], axis=1), rhs, "psk,psn->pkn", passes[4])
    q_eff = r_bd + top[:, :, :c2]
    o_intra = top[:, :, c2:]
    m_mat = eye * pc + bot[:, :, :c2]
    g_mat = bot[:, :, c2:]

    h0 = h_ref[...]
    seq_out = _bmm(jnp.concatenate([q_eff, m_mat], axis=1), h0, "pij,pjk->pik", passes[5])
    o_bd = seq_out[:, :c2, :] + o_intra
    h_ref[...] = seq_out[:, c2:, :] + g_mat
    o = jnp.where(lo, o_bd[:, :c, :], o_bd[:, c:, :])

    rk = rk_ref[...]
    lnw = lnw_ref[...]
    lnb = lnb_ref[...]
    g = g_ref[...]
    lo2 = lax.broadcasted_iota(jnp.int32, (c, LANES), 1) < RWKV_HEAD
    inv_n = 1.0 / RWKV_HEAD
    for t in range(RWKV_DIM // LANES):
        sl = slice(t * LANES, (t + 1) * LANES)
        ot = o[t]
        s0, s1 = _seg2(ot)
        cen = ot - jnp.where(lo2, s0, s1) * inv_n
        q0, q1 = _seg2(cen * cen)
        on = cen * lax.rsqrt(jnp.where(lo2, q0, q1) * inv_n + GN_EPS) * lnw[:, sl] + lnb[:, sl]
        b0, b1 = _seg2(r[:, sl] * k[:, sl] * rk[:, sl])
        on = on + jnp.where(lo2, b0, b1) * v[:, sl]
        o_ref[:, sl] = (on * g[:, sl]).astype(o_ref.dtype)


def _rwkv_scan(r, k, v, kk, a, lw, g, rk, lnw, lnb, batch, seq, passes):
    n, d = r.shape
    nck = seq // CHUNK
    blk = pl.BlockSpec((CHUNK, d), lambda bi, ci: (bi * nck + ci, 0))
    full = lambda x: pl.BlockSpec(x.shape, lambda bi, ci: (0,) * x.ndim)
    return pl.pallas_call(
        functools.partial(_rwkv_scan_kernel, passes=passes),
        grid=(batch, nck),
        in_specs=[blk] * 7 + [full(rk), full(lnw), full(lnb)],
        out_specs=blk,
        out_shape=jax.ShapeDtypeStruct((n, d), BF16),
        scratch_shapes=[pltpu.VMEM((d // LANES, LANES, LANES), F32)],
        compiler_params=_cparams(("parallel", "arbitrary")),
        name="rwkv_scan",
    )(r, k, v, kk, a, lw, g, rk, lnw, lnb)


def _resident(a):
    return pl.BlockSpec(a.shape, lambda i: (0,) * a.ndim, pipeline_mode=pl.Buffered(1))


def _out_proj_kernel(x_ref, oa_ref, ob_ref, w_ref, o_ref, w16_ref):
    @pl.when(pl.program_id(0) == 0)
    def _():
        w16_ref[...] = w_ref[...].astype(BF16)

    ka = oa_ref.shape[1]
    o_ref[...] = x_ref[...] + _dot(oa_ref[...], w16_ref[:ka, :]) + _dot(ob_ref[...], w16_ref[ka:, :])


def _out_proj(x, oa, ob, w, tm):
    n, d = x.shape
    row_blk = lambda a: pl.BlockSpec((tm, a.shape[1]), lambda i: (i, 0))
    return pl.pallas_call(
        _out_proj_kernel,
        grid=(n // tm,),
        in_specs=[row_blk(x), row_blk(oa), row_blk(ob), _resident(w)],
        out_specs=row_blk(x),
        out_shape=jax.ShapeDtypeStruct((n, d), F32),
        scratch_shapes=[pltpu.VMEM(w.shape, BF16)],
        compiler_params=_cparams(("arbitrary",)),
        name="out_proj",
    )(x, oa, ob, w)


def _ple_kernel(x_ref, g_ref, p_ref, wg_ref, wp_ref, o_ref, wg16_ref, wp16_ref):
    @pl.when(pl.program_id(0) == 0)
    def _():
        wg16_ref[...] = wg_ref[...].astype(BF16)
        wp16_ref[...] = wp_ref[...].astype(BF16)

    x = x_ref[...]
    gate = jax.nn.sigmoid(_dot(_rms(x, g_ref[...]).astype(BF16), wg16_ref[...]))
    o_ref[...] = x + gate * _dot(p_ref[...].astype(BF16), wp16_ref[...])


def _ple(x, g, p, wg, wp, tm):
    n, d = x.shape
    row_blk = lambda a: pl.BlockSpec((tm, a.shape[1]), lambda i: (i, 0))
    return pl.pallas_call(
        _ple_kernel,
        grid=(n // tm,),
        in_specs=[row_blk(x), _resident(g), row_blk(p), _resident(wg), _resident(wp)],
        out_specs=row_blk(x),
        out_shape=jax.ShapeDtypeStruct((n, d), F32),
        scratch_shapes=[pltpu.VMEM(wg.shape, BF16), pltpu.VMEM(wp.shape, BF16)],
        compiler_params=_cparams(("arbitrary",)),
        name="ple",
    )(x, g, p, wg, wp)


def _tile(n, pref):
    t = min(n, pref)
    assert n % t == 0, (n, t)
    return t


def _layer(x, p, norm_ffn1, w1_gate, w1_up, w1_down, norm_mix, w_in, q_a_norm, w_q_b, kv_a_norm, w_kv_b,
           q_norm, k_norm, mu_shift, w0, w_w2, a0, w_a2, w_g2, k_k, k_a, r_k, ln_x_w, ln_x_b, w_out,
           norm_ffn2, w2_gate, w2_up, w2_down, norm_ple, w_ple_gate, w_ple_proj):
    batch, seq, d = x.shape
    n = batch * seq
    row = lambda a: a.reshape(1, -1)
    bf = lambda a: a.astype(BF16)
    mla_in = Q_LORA + KV_LORA + QK_ROPE
    dff = w1_gate.shape[1]
    tf = _tile(dff, 256)
    tm_ffn = _tile(n, 1024)

    xf = x.reshape(n, d)
    x1 = _ffn(xf, row(norm_ffn1), w1_gate, w1_up, w1_down, tm_ffn, tf)

    w_mla = bf(jnp.concatenate([w_in[:, :mla_in], w_in[:, mla_in - QK_ROPE:mla_in]], axis=1))
    wq = w_q_b.reshape(Q_LORA, MLA_HEADS, QK_HEAD)
    wqb = bf(jnp.concatenate([wq[:, :, :QK_NOPE].reshape(Q_LORA, -1), wq[:, :, QK_NOPE:].reshape(Q_LORA, -1)], axis=1))
    inv_freq = 1.0 / (ROPE_BASE ** (jnp.arange(0, QK_ROPE, 2, dtype=F32) / QK_ROPE))
    ang = jnp.arange(seq, dtype=F32)[:, None] * inv_freq[None, :]
    cos, sin = jnp.cos(ang), jnp.sin(ang)
    cos_t = jnp.concatenate([cos, cos, cos, cos], axis=1)
    sin_t = jnp.concatenate([-sin, sin, -sin, sin], axis=1)
    pair = lambda a: row(jnp.concatenate([a, a]))
    tm_prep = _tile(seq, 256)
    q, k, v = _mla_prep(x1, row(norm_mix), w_mla, row(q_a_norm), wqb, row(kv_a_norm), bf(w_kv_b),
                        row(q_norm[:QK_NOPE]), pair(q_norm[QK_NOPE:]), row(k_norm[:QK_NOPE]),
                        pair(k_norm[QK_NOPE:]), cos_t, sin_t, batch, seq, tm_prep)
    o_mla = _flash(q, k, v, _tile(seq, 1024), _tile(seq, 512)).reshape(n, MLA_HEADS * V_HEAD)

    z = _norm_mm(x1, row(norm_mix), bf(w_in[:, mla_in:]), _tile(n, 512), 26 * LANES)
    zero = jnp.zeros_like(w_w2)
    ww = bf(jnp.concatenate([w_w2, zero], axis=0))
    wa = bf(jnp.concatenate([zero, w_a2], axis=0))
    rw = _rwkv_prep(z, row(mu_shift), row(w0), ww, row(a0), wa, bf(w_g2), row(k_k), row(k_a), seq, tm_prep)
    o_rwkv = _rwkv_scan(*rw, row(r_k), row(ln_x_w), row(ln_x_b), batch, seq, SCAN_PASSES)

    x2 = _out_proj(x1, o_mla, o_rwkv, w_out, _tile(n, 512))
    x3 = _ffn(x2, row(norm_ffn2), w2_gate, w2_up, w2_down, tm_ffn, tf)
    out = _ple(x3, row(norm_ple), p.reshape(n, -1), w_ple_gate, w_ple_proj, _tile(n, 256))
    return out.reshape(batch, seq, d)


def kernel(x, p, norm_ffn1, w1_gate, w1_up, w1_down, norm_mix, w_in, q_a_norm, w_q_b, kv_a_norm, w_kv_b, q_norm, k_norm, mu_shift, w0, w_w2, a0, w_a2, w_g2, k_k, k_a, r_k, ln_x_w, ln_x_b, w_out, norm_ffn2, w2_gate, w2_up, w2_down, norm_ple, w_ple_gate, w_ple_proj):
    depth = p.shape[0]
    for i in range(depth):
        x = _layer(x, p[i], norm_ffn1[i], w1_gate[i], w1_up[i], w1_down[i], norm_mix[i], w_in[i], q_a_norm[i],
                   w_q_b[i], kv_a_norm[i], w_kv_b[i], q_norm[i], k_norm[i], mu_shift[i], w0[i], w_w2[i], a0[i],
                   w_a2[i], w_g2[i], k_k[i], k_a[i], r_k[i], ln_x_w[i], ln_x_b[i], w_out[i], norm_ffn2[i],
                   w2_gate[i], w2_up[i], w2_down[i], norm_ple[i], w_ple_gate[i], w_ple_proj[i])
    return x
```

```python
import functools
import math

import jax
import jax.numpy as jnp
from jax import lax
from jax.experimental import pallas as pl
from jax.experimental.pallas import tpu as pltpu

F32 = jnp.float32
BF16 = jnp.bfloat16

EPS = 1e-6
GN_EPS = 64e-5
MLA_HEADS = 8
QK_NOPE = 128
QK_ROPE = 64
QK_HEAD = QK_NOPE + QK_ROPE
V_HEAD = 128
Q_LORA = 512
KV_LORA = 256
ROPE_BASE = 10000.0
RWKV_HEAD = 64
RWKV_HEADS = 16
RWKV_DIM = RWKV_HEADS * RWKV_HEAD
LANES = 128
CHUNK = 64
VMEM_LIMIT = 56 * 1024 * 1024
SCAN_PASSES = (1, 1, 1, 1, 1, 1)


def _cparams(sem):
    return pltpu.CompilerParams(dimension_semantics=sem, vmem_limit_bytes=VMEM_LIMIT)


def _dot(a, b):
    return jnp.dot(a, b, preferred_element_type=F32)


def _rms(x, g):
    return x * lax.rsqrt(jnp.mean(x * x, axis=-1, keepdims=True) + EPS) * g


def _ffn_kernel(x_ref, g_ref, wg_ref, wu_ref, wd_ref, o_ref, h_ref):
    j = pl.program_id(1)

    @pl.when(j == 0)
    def _():
        h_ref[...] = _rms(x_ref[...], g_ref[...]).astype(BF16)
        o_ref[...] = jnp.zeros_like(o_ref)

    h = h_ref[...]
    gate = _dot(h, wg_ref[...].astype(BF16))
    up = _dot(h, wu_ref[...].astype(BF16))
    act = (gate * jax.nn.sigmoid(gate) * up).astype(BF16)
    o_ref[...] += _dot(act, wd_ref[...].astype(BF16))

    @pl.when(j == pl.num_programs(1) - 1)
    def _():
        o_ref[...] = x_ref[...] + 0.5 * o_ref[...]


def _ffn(x, g, wg, wu, wd, tm, tf):
    n, d = x.shape
    dff = wg.shape[1]
    return pl.pallas_call(
        _ffn_kernel,
        grid=(n // tm, dff // tf),
        in_specs=[
            pl.BlockSpec((tm, d), lambda i, j: (i, 0)),
            pl.BlockSpec((1, d), lambda i, j: (0, 0)),
            pl.BlockSpec((d, tf), lambda i, j: (0, j)),
            pl.BlockSpec((d, tf), lambda i, j: (0, j)),
            pl.BlockSpec((tf, d), lambda i, j: (j, 0)),
        ],
        out_specs=pl.BlockSpec((tm, d), lambda i, j: (i, 0)),
        out_shape=jax.ShapeDtypeStruct((n, d), F32),
        scratch_shapes=[pltpu.VMEM((tm, d), BF16)],
        compiler_params=_cparams(("parallel", "arbitrary")),
        name="ffn",
    )(x, g, wg, wu, wd)


def _norm_mm_kernel(x_ref, g_ref, w_ref, o_ref, h_ref):
    @pl.when(pl.program_id(1) == 0)
    def _():
        h_ref[...] = _rms(x_ref[...], g_ref[...]).astype(BF16)

    o_ref[...] = _dot(h_ref[...], w_ref[...])


def _norm_mm(x, g, w, tm, tn):
    n, d = x.shape
    nc = w.shape[1]
    return pl.pallas_call(
        _norm_mm_kernel,
        grid=(n // tm, nc // tn),
        in_specs=[
            pl.BlockSpec((tm, d), lambda i, j: (i, 0)),
            pl.BlockSpec((1, d), lambda i, j: (0, 0)),
            pl.BlockSpec((d, tn), lambda i, j: (0, j)),
        ],
        out_specs=pl.BlockSpec((tm, tn), lambda i, j: (i, j)),
        out_shape=jax.ShapeDtypeStruct((n, nc), F32),
        scratch_shapes=[pltpu.VMEM((tm, d), BF16)],
        compiler_params=_cparams(("parallel", "arbitrary")),
        name="norm_mm",
    )(x, g, w)


def _seg2(x):
    lo = lax.broadcasted_iota(jnp.int32, x.shape, 1) < RWKV_HEAD
    s0 = jnp.sum(jnp.where(lo, x, 0.0), axis=-1, keepdims=True)
    s1 = jnp.sum(jnp.where(lo, 0.0, x), axis=-1, keepdims=True)
    return s0, s1


def _rope_pair(y, cos_t, sin_t):
    lane = lax.broadcasted_iota(jnp.int32, y.shape, 1)
    first = (lane % QK_ROPE) < (QK_ROPE // 2)
    rot = jnp.where(first, pltpu.roll(y, LANES - QK_ROPE // 2, axis=1), pltpu.roll(y, QK_ROPE // 2, axis=1))
    return y * cos_t + rot * sin_t


def _mla_prep_kernel(x_ref, gmix_ref, win_ref, qag_ref, wqb_ref, kvag_ref, wkvb_ref,
                     gqn_ref, gqp_ref, gkn_ref, gkp_ref, cos_ref, sin_ref,
                     q_ref, k_ref, v_ref):
    scale = 1.0 / math.sqrt(QK_HEAD)
    h = _rms(x_ref[...], gmix_ref[...]).astype(BF16)
    z = _dot(h, win_ref[...])
    q_lat = z[:, :Q_LORA]
    kv_lat = z[:, Q_LORA:Q_LORA + KV_LORA]
    kpe2 = z[:, Q_LORA + KV_LORA:]
    qf = _dot(_rms(q_lat, qag_ref[...]).astype(BF16), wqb_ref[...])
    kvf = _dot(_rms(kv_lat, kvag_ref[...]).astype(BF16), wkvb_ref[...])
    cos_t = cos_ref[...]
    sin_t = sin_ref[...]
    lo = lax.broadcasted_iota(jnp.int32, cos_t.shape, 1) < QK_ROPE
    kpe_ss, _ = _seg2(kpe2 * kpe2)
    nope_w = MLA_HEADS * QK_NOPE
    for j in range(MLA_HEADS // 2):
        qp = qf[:, nope_w + j * LANES: nope_w + (j + 1) * LANES]
        qs0, qs1 = _seg2(qp * qp)
        rs_q = []
        rs_k = []
        for e, qs in ((0, qs0), (1, qs1)):
            hd = 2 * j + e
            qn = qf[:, hd * QK_NOPE:(hd + 1) * QK_NOPE]
            rq = lax.rsqrt((jnp.sum(qn * qn, axis=-1, keepdims=True) + qs) * (1.0 / QK_HEAD) + EPS)
            q_ref[0, hd, :, :QK_NOPE] = (qn * rq * gqn_ref[...] * scale).astype(BF16)
            rs_q.append(rq)
            kn = kvf[:, hd * 2 * QK_NOPE: hd * 2 * QK_NOPE + QK_NOPE]
            rk = lax.rsqrt((jnp.sum(kn * kn, axis=-1, keepdims=True) + kpe_ss) * (1.0 / QK_HEAD) + EPS)
            k_ref[0, hd, :, :QK_NOPE] = (kn * rk * gkn_ref[...]).astype(BF16)
            rs_k.append(rk)
            v_ref[0, hd] = kvf[:, hd * 2 * QK_NOPE + QK_NOPE:(hd + 1) * 2 * QK_NOPE].astype(BF16)
        yq = _rope_pair(qp * jnp.where(lo, rs_q[0], rs_q[1]) * gqp_ref[...], cos_t, sin_t) * scale
        yk = _rope_pair(kpe2 * jnp.where(lo, rs_k[0], rs_k[1]) * gkp_ref[...], cos_t, sin_t)
        q_ref[0, 2 * j, :, QK_NOPE:] = jnp.where(lo, yq, 0.0).astype(BF16)
        q_ref[0, 2 * j + 1, :, QK_NOPE:] = jnp.where(lo, 0.0, yq).astype(BF16)
        k_ref[0, 2 * j, :, QK_NOPE:] = jnp.where(lo, yk, 0.0).astype(BF16)
        k_ref[0, 2 * j + 1, :, QK_NOPE:] = jnp.where(lo, 0.0, yk).astype(BF16)


def _mla_prep(x, gmix, win, qag, wqb, kvag, wkvb, gqn, gqp, gkn, gkp, cos_t, sin_t, batch, seq, tm):
    n, d = x.shape
    spt = seq // tm
    full = lambda a: pl.BlockSpec(a.shape, lambda i: (0,) * a.ndim)
    qk_w = 2 * LANES
    return pl.pallas_call(
        _mla_prep_kernel,
        grid=(n // tm,),
        in_specs=[
            pl.BlockSpec((tm, d), lambda i: (i, 0)),
            full(gmix), full(win), full(qag), full(wqb), full(kvag), full(wkvb),
            full(gqn), full(gqp), full(gkn), full(gkp),
            pl.BlockSpec((tm, LANES), lambda i: (i % spt, 0)),
            pl.BlockSpec((tm, LANES), lambda i: (i % spt, 0)),
        ],
        out_specs=[
            pl.BlockSpec((1, MLA_HEADS, tm, qk_w), lambda i: (i // spt, 0, i % spt, 0)),
            pl.BlockSpec((1, MLA_HEADS, tm, qk_w), lambda i: (i // spt, 0, i % spt, 0)),
            pl.BlockSpec((1, MLA_HEADS, tm, V_HEAD), lambda i: (i // spt, 0, i % spt, 0)),
        ],
        out_shape=[
            jax.ShapeDtypeStruct((batch, MLA_HEADS, seq, qk_w), BF16),
            jax.ShapeDtypeStruct((batch, MLA_HEADS, seq, qk_w), BF16),
            jax.ShapeDtypeStruct((batch, MLA_HEADS, seq, V_HEAD), BF16),
        ],
        compiler_params=_cparams(("parallel",)),
        name="mla_prep",
    )(x, gmix, win, qag, wqb, kvag, wkvb, gqn, gqp, gkn, gkp, cos_t, sin_t)


def _flash_kernel(q_ref, k_ref, v_ref, o_ref, m_ref, l_ref, acc_ref, *, tk, nsub):
    qi = pl.program_id(2)
    m_ref[...] = jnp.full_like(m_ref, -1e30)
    l_ref[...] = jnp.zeros_like(l_ref)
    acc_ref[...] = jnp.zeros_like(acc_ref)
    rep = tk // LANES
    causal = lax.broadcasted_iota(jnp.int32, (tk, tk), 1) <= lax.broadcasted_iota(jnp.int32, (tk, tk), 0)

    def step(sub, j, diag):
        rows = pl.ds(sub * tk, tk)
        kv = pl.ds(pl.multiple_of(j * tk, tk), tk)
        s = lax.dot_general(q_ref[0, 0, rows, :], k_ref[0, 0, kv, :], (((1,), (1,)), ((), ())),
                            preferred_element_type=F32)
        if diag:
            s = jnp.where(causal, s, -1e30)
        m_prev = m_ref[rows, :]
        m_new = jnp.maximum(m_prev, jnp.max(s, axis=-1, keepdims=True))
        alpha = jnp.exp(m_prev - m_new)
        p = jnp.exp(s - jnp.concatenate([m_new] * rep, axis=1))
        l_ref[rows, :] = alpha * l_ref[rows, :] + jnp.sum(p, axis=-1, keepdims=True)
        acc_ref[rows, :] = alpha * acc_ref[rows, :] + _dot(p.astype(BF16), v_ref[0, 0, kv, :])
        m_ref[rows, :] = m_new

    def body(j, carry):
        for sub in range(nsub):
            step(sub, j, False)
        return carry

    lax.fori_loop(0, qi * nsub, body, 0)
    for e in range(nsub):
        for sub in range(e, nsub):
            step(sub, qi * nsub + e, sub == e)
    o_ref[0] = (acc_ref[...] / l_ref[...]).astype(o_ref.dtype)


def _flash(q, k, v, tq, tk):
    b, hh, s, dk = q.shape
    dv = v.shape[-1]
    assert dv == LANES and tq % tk == 0
    return pl.pallas_call(
        functools.partial(_flash_kernel, tk=tk, nsub=tq // tk),
        grid=(b, hh, s // tq),
        in_specs=[
            pl.BlockSpec((1, 1, tq, dk), lambda bi, h, qi: (bi, h, qi, 0)),
            pl.BlockSpec((1, 1, s, dk), lambda bi, h, qi: (bi, h, 0, 0)),
            pl.BlockSpec((1, 1, s, dv), lambda bi, h, qi: (bi, h, 0, 0)),
        ],
        out_specs=pl.BlockSpec((1, tq, dv), lambda bi, h, qi: (bi, qi, h)),
        out_shape=jax.ShapeDtypeStruct((b, s, hh * dv), BF16),
        scratch_shapes=[pltpu.VMEM((tq, LANES), F32), pltpu.VMEM((tq, LANES), F32), pltpu.VMEM((tq, dv), F32)],
        compiler_params=_cparams(("parallel", "parallel", "arbitrary")),
        name="flash",
    )(q, k, v)


def _rwkv_prep_kernel(z_ref, zp_ref, mu_ref, w0_ref, ww_ref, a0_ref, wa_ref, wg_ref, kk_ref, ka_ref,
                      r_o, k_o, v_o, kk_o, a_o, lw_o, g_o, *, seq):
    tm = z_ref.shape[0]
    z = z_ref[...]
    first = (pl.program_id(0) * tm) % seq == 0
    prow = jnp.where(first, 0.0, zp_ref[7:8, :])
    rowid = lax.broadcasted_iota(jnp.int32, z.shape, 0)
    prev = jnp.where(rowid == 0, prow, pltpu.roll(z, 1, axis=0))
    zs = z + (prev - z) * mu_ref[...]
    d = RWKV_DIM
    r = zs[:, :d]
    k = zs[:, d:2 * d]
    v = zs[:, 2 * d:3 * d]
    wa_lo = zs[:, 3 * d:3 * d + LANES]
    g_lo = zs[:, 3 * d + LANES:]
    wpre = w0_ref[...] + _dot(jnp.tanh(wa_lo).astype(BF16), ww_ref[...])
    y = -wpre
    w = -(jnp.maximum(y, 0.0) + jnp.log(1.0 + jnp.exp(-jnp.abs(y)))) - 0.5
    a = jax.nn.sigmoid(a0_ref[...] + _dot(wa_lo.astype(BF16), wa_ref[...]))
    g = _dot(jax.nn.sigmoid(g_lo).astype(BF16), wg_ref[...])
    r_o[...] = r
    v_o[...] = v
    a_o[...] = a
    g_o[...] = g
    lw_o[...] = -jnp.exp(w)
    k_o[...] = k * (1.0 + (a - 1.0) * ka_ref[...])
    kk = k * kk_ref[...]
    lo = lax.broadcasted_iota(jnp.int32, (tm, LANES), 1) < RWKV_HEAD
    for t in range(d // LANES):
        kt = kk[:, t * LANES:(t + 1) * LANES]
        s0, s1 = _seg2(kt * kt)
        nrm = jnp.where(lo, jnp.maximum(jnp.sqrt(s0), 1e-12), jnp.maximum(jnp.sqrt(s1), 1e-12))
        kk_o[:, t * LANES:(t + 1) * LANES] = kt / nrm


def _rwkv_prep(z, mu, w0, ww, a0, wa, wg, kkp, kap, seq, tm):
    n, zin = z.shape
    d = RWKV_DIM
    full = lambda a: pl.BlockSpec(a.shape, lambda i: (0,) * a.ndim)
    out = pl.BlockSpec((tm, d), lambda i: (i, 0))
    return pl.pallas_call(
        functools.partial(_rwkv_prep_kernel, seq=seq),
        grid=(n // tm,),
        in_specs=[
            pl.BlockSpec((tm, zin), lambda i: (i, 0)),
            pl.BlockSpec((8, zin), lambda i: (jnp.maximum(i * (tm // 8) - 1, 0), 0)),
            full(mu), full(w0), full(ww), full(a0), full(wa), full(wg), full(kkp), full(kap),
        ],
        out_specs=[out] * 7,
        out_shape=[jax.ShapeDtypeStruct((n, d), F32)] * 7,
        compiler_params=_cparams(("parallel",)),
        name="rwkv_prep",
    )(z, z, mu, w0, ww, a0, wa, wg, kkp, kap)


def _split(x):
    hi = x.astype(BF16)
    return hi, (x - hi.astype(F32)).astype(BF16)


def _bmm(a, b, spec, passes):
    mm = lambda u, w: jnp.einsum(spec, u, w, preferred_element_type=F32)
    if passes == 1:
        return mm(a.astype(BF16), b.astype(BF16))
    ah, al = _split(a)
    bh, bl = _split(b)
    return mm(ah, bh) + mm(al, bh) + mm(ah, bl)


def _tiles(x):
    return jnp.stack([x[:, t * LANES:(t + 1) * LANES] for t in range(x.shape[1] // LANES)])


def _rwkv_scan_kernel(r_ref, k_ref, v_ref, kk_ref, a_ref, lw_ref, g_ref, rk_ref, lnw_ref, lnb_ref,
                      o_ref, h_ref, *, passes):
    c = CHUNK
    c2 = 2 * c

    @pl.when(pl.program_id(1) == 0)
    def _():
        h_ref[...] = jnp.zeros_like(h_ref)

    lw = lw_ref[...]
    ti = lax.broadcasted_iota(jnp.int32, (c, c), 0)
    si = lax.broadcasted_iota(jnp.int32, (c, c), 1)
    tril = jnp.where(si <= ti, 1.0, 0.0)
    lwh, lwl = _split(lw)
    lwl2 = (lw - lwh.astype(F32) - lwl.astype(F32)).astype(BF16)
    trilb = tril.astype(BF16)
    cum = _dot(trilb, lwh) + _dot(trilb, lwl) + _dot(trilb, lwl2)
    p = jnp.exp(cum)
    pinv = jnp.exp(-cum)
    pprev = jnp.exp(cum - lw)
    pend = jnp.exp(cum[c - 1:c, :] - cum)
    r = r_ref[...]
    k = k_ref[...]
    v = v_ref[...]
    kk = kk_ref[...]
    b = kk * a_ref[...]
    at = _tiles(-kk * pprev)
    bt = _tiles(b * pinv)
    kt = _tiles(k * pinv)
    rt = _tiles(r * p)
    bh = _tiles(b * pend)
    kh = _tiles(k * pend)
    vt = _tiles(v)
    pc = _tiles(jnp.exp(cum[c - 1:c, :]))

    lo = lax.broadcasted_iota(jnp.int32, (1, 1, LANES), 2) < RWKV_HEAD

    def bd(x):
        return jnp.concatenate([jnp.where(lo, x, 0.0), jnp.where(lo, 0.0, x)], axis=1)

    a_bd, b_bd, k_bd, r_bd, v_bd = bd(at), bd(bt), bd(kt), bd(rt), bd(vt)
    bh_bd, kh_bd = bd(bh), bd(kh)

    amat = _bmm(jnp.concatenate([a_bd, r_bd], axis=1), jnp.concatenate([b_bd, k_bd], axis=1),
                "pil,pjl->pij", passes[0])
    ri = lax.broadcasted_iota(jnp.int32, (1, c2, c2), 1)
    ci = lax.broadcasted_iota(jnp.int32, (1, c2, c2), 2)
    same = (ri // c) == (ci // c)
    strict = same & ((ri % c) > (ci % c))
    incl = same & ((ri % c) >= (ci % c))
    a_ab = jnp.where(strict, amat[:, :c2, :c2], 0.0)
    a_ak = jnp.where(strict, amat[:, :c2, c2:], 0.0)
    a_rb = jnp.where(incl, amat[:, c2:, :c2], 0.0)
    a_rk = jnp.where(incl, amat[:, c2:, c2:], 0.0)

    eye = jnp.where(ri == ci, 1.0, 0.0)
    tinv = eye + a_ab
    apow = a_ab
    for _ in range(int(math.log2(c)) - 1):
        apow = _bmm(apow, apow, "pij,pjk->pik", passes[1])
        tinv = tinv + _bmm(tinv, apow, "pij,pjk->pik", passes[1])

    akv = _bmm(a_ak, v_bd, "pij,pjk->pik", passes[2])
    wu = _bmm(tinv, jnp.concatenate([a_bd, akv], axis=2), "pij,pjk->pik", passes[3])
    rhs = jnp.concatenate([wu, jnp.concatenate([jnp.zeros_like(v_bd), v_bd], axis=2)], axis=1)
    top = _bmm(jnp.concatenate([a_rb, a_rk], axis=2), rhs, "pij,pjk->pik", passes[4])
    bot = _bmm(jnp.concatenate([bh_bd, kh_bd], axis=1), rhs, "psk,psn->pkn", passes[4])
    q_eff = r_bd + top[:, :, :c2]
    o_intra = top[:, :, c2:]
    m_mat = eye * pc + bot[:, :, :c2]
    g_mat = bot[:, :, c2:]

    h0 = h_ref[...]
    seq_out = _bmm(jnp.concatenate([q_eff, m_mat], axis=1), h0, "pij,pjk->pik", passes[5])
    o_bd = seq_out[:, :c2, :] + o_intra
    h_ref[...] = seq_out[:, c2:, :] + g_mat
    o = jnp.where(lo, o_bd[:, :c, :], o_bd[:, c:, :])

    rk = rk_ref[...]
    lnw = lnw_ref[...]
    lnb = lnb_ref[...]
    g = g_ref[...]
    lo2 = lax.broadcasted_iota(jnp.int32, (c, LANES), 1) < RWKV_HEAD
    inv_n = 1.0 / RWKV_HEAD
    for t in range(RWKV_DIM // LANES):
        sl = slice(t * LANES, (t + 1) * LANES)
        ot = o[t]
        s0, s1 = _seg2(ot)
        cen = ot - jnp.where(lo2, s0, s1) * inv_n
        q0, q1 = _seg2(cen * cen)
        on = cen * lax.rsqrt(jnp.where(lo2, q0, q1) * inv_n + GN_EPS) * lnw[:, sl] + lnb[:, sl]
        b0, b1 = _seg2(r[:, sl] * k[:, sl] * rk[:, sl])
        on = on + jnp.where(lo2, b0, b1) * v[:, sl]
        o_ref[:, sl] = (on * g[:, sl]).astype(o_ref.dtype)


def _rwkv_scan(r, k, v, kk, a, lw, g, rk, lnw, lnb, batch, seq, passes):
    n, d = r.shape
    nck = seq // CHUNK
    blk = pl.BlockSpec((CHUNK, d), lambda bi, ci: (bi * nck + ci, 0))
    full = lambda x: pl.BlockSpec(x.shape, lambda bi, ci: (0,) * x.ndim)
    return pl.pallas_call(
        functools.partial(_rwkv_scan_kernel, passes=passes),
        grid=(batch, nck),
        in_specs=[blk] * 7 + [full(rk), full(lnw), full(lnb)],
        out_specs=blk,
        out_shape=jax.ShapeDtypeStruct((n, d), BF16),
        scratch_shapes=[pltpu.VMEM((d // LANES, LANES, LANES), F32)],
        compiler_params=_cparams(("parallel", "arbitrary")),
        name="rwkv_scan",
    )(r, k, v, kk, a, lw, g, rk, lnw, lnb)


def _bmm1(a, b, spec):
    return jnp.einsum(spec, a.astype(BF16), b.astype(BF16), preferred_element_type=F32)


def _rwkv_kernel(x_ref, gmix_ref, w_ref, mu_ref, w0_ref, ww_ref, a0_ref, wa_ref, wg_ref, kkp_ref, kap_ref,
                 rk_ref, lnw_ref, lnb_ref, o_ref, h_ref, zlast_ref):
    c = CHUNK
    c2 = 2 * c
    d = RWKV_DIM
    nt = d // LANES
    tb = x_ref.shape[0]
    nck = tb // c

    @pl.when(pl.program_id(1) == 0)
    def _():
        h_ref[...] = jnp.zeros_like(h_ref)
        zlast_ref[...] = jnp.zeros_like(zlast_ref)

    z = _dot(_rms(x_ref[...], gmix_ref[...]).astype(BF16), w_ref[...])
    rowid = lax.broadcasted_iota(jnp.int32, z.shape, 0)
    prev = jnp.where(rowid == 0, zlast_ref[...], pltpu.roll(z, 1, axis=0))
    zlast_ref[...] = z[tb - 1:tb, :]
    zs = z + (prev - z) * mu_ref[...]
    r = zs[:, :d]
    k0 = zs[:, d:2 * d]
    v = zs[:, 2 * d:3 * d]
    wa_lo = zs[:, 3 * d:3 * d + LANES]
    g_lo = zs[:, 3 * d + LANES:]
    y = -(w0_ref[...] + _dot(jnp.tanh(wa_lo).astype(BF16), ww_ref[...]))
    w = -(jnp.maximum(y, 0.0) + jnp.log(1.0 + jnp.exp(-jnp.abs(y)))) - 0.5
    lw = -jnp.exp(w)
    a = jax.nn.sigmoid(a0_ref[...] + _dot(wa_lo.astype(BF16), wa_ref[...]))
    g = _dot(jax.nn.sigmoid(g_lo).astype(BF16), wg_ref[...])
    k = k0 * (1.0 + (a - 1.0) * kap_ref[...])
    kk0 = k0 * kkp_ref[...]
    lo2 = lax.broadcasted_iota(jnp.int32, (tb, LANES), 1) < RWKV_HEAD
    kk_t = []
    for t in range(nt):
        kt = kk0[:, t * LANES:(t + 1) * LANES]
        s0, s1 = _seg2(kt * kt)
        kk_t.append(kt / jnp.where(lo2, jnp.maximum(jnp.sqrt(s0), 1e-12), jnp.maximum(jnp.sqrt(s1), 1e-12)))
    kk = jnp.concatenate(kk_t, axis=1)

    ti = lax.broadcasted_iota(jnp.int32, (tb, tb), 0)
    si = lax.broadcasted_iota(jnp.int32, (tb, tb), 1)
    tril = jnp.where((si <= ti) & (si // c == ti // c), 1.0, 0.0).astype(BF16)
    lw1 = lw.astype(BF16)
    lw2 = (lw - lw1.astype(F32)).astype(BF16)
    lw3 = (lw - lw1.astype(F32) - lw2.astype(F32)).astype(BF16)
    cum = _dot(tril, lw1) + _dot(tril, lw2) + _dot(tril, lw3)
    cend = jnp.concatenate([jnp.broadcast_to(cum[(ci + 1) * c - 1:(ci + 1) * c, :], (c, d)) for ci in range(nck)],
                           axis=0)
    p = jnp.exp(cum)
    pinv = jnp.exp(-cum)
    pprev = jnp.exp(cum - lw)
    pend = jnp.exp(cend - cum)
    b = kk * a

    def tiles(x):
        return jnp.stack([x[ci * c:(ci + 1) * c, t * LANES:(t + 1) * LANES]
                          for ci in range(nck) for t in range(nt)])

    lo = lax.broadcasted_iota(jnp.int32, (1, 1, LANES), 2) < RWKV_HEAD

    def bd(x):
        x = tiles(x).astype(BF16)
        zero = jnp.zeros_like(x)
        return jnp.concatenate([jnp.where(lo, x, zero), jnp.where(lo, zero, x)], axis=1)

    a_bd, b_bd, k_bd, r_bd = bd(-kk * pprev), bd(b * pinv), bd(k * pinv), bd(r * p)
    v_bd, bh_bd, kh_bd = bd(v), bd(b * pend), bd(k * pend)
    pc = jnp.stack([jnp.exp(cend[ci * c:ci * c + 1, t * LANES:(t + 1) * LANES])
                    for ci in range(nck) for t in range(nt)])

    amat = _bmm1(jnp.concatenate([a_bd, r_bd], axis=1), jnp.concatenate([b_bd, k_bd], axis=1), "pil,pjl->pij")
    ri = lax.broadcasted_iota(jnp.int32, (1, c2, c2), 1)
    ci_ = lax.broadcasted_iota(jnp.int32, (1, c2, c2), 2)
    same = (ri // c) == (ci_ // c)
    strict = same & ((ri % c) > (ci_ % c))
    incl = same & ((ri % c) >= (ci_ % c))
    a_ab = jnp.where(strict, amat[:, :c2, :c2], 0.0)
    a_ak = jnp.where(strict, amat[:, :c2, c2:], 0.0)
    a_rb = jnp.where(incl, amat[:, c2:, :c2], 0.0)
    a_rk = jnp.where(incl, amat[:, c2:, c2:], 0.0)

    eye = jnp.where(ri == ci_, 1.0, 0.0)
    tinv = eye + a_ab
    apow = a_ab
    for _ in range(int(math.log2(c)) - 1):
        apow = _bmm1(apow, apow, "pij,pjk->pik")
        tinv = tinv + _bmm1(tinv, apow, "pij,pjk->pik")

    akv = _bmm1(a_ak, v_bd, "pij,pjk->pik")
    wu = _bmm1(tinv, jnp.concatenate([a_bd, akv.astype(BF16)], axis=2), "pij,pjk->pik")
    rhs = jnp.concatenate([wu.astype(BF16), jnp.concatenate([jnp.zeros_like(v_bd), v_bd], axis=2)], axis=1)
    top = _bmm1(jnp.concatenate([a_rb, a_rk], axis=2), rhs, "pij,pjk->pik")
    bot = _bmm1(jnp.concatenate([bh_bd, kh_bd], axis=1), rhs, "psk,psn->pkn")
    q_eff = r_bd.astype(F32) + top[:, :, :c2]
    o_intra = top[:, :, c2:]
    m_mat = eye * pc + bot[:, :, :c2]
    g_mat = bot[:, :, c2:]

    hs = h_ref[...]
    o_ck = []
    for ci in range(nck):
        sl = slice(ci * nt, (ci + 1) * nt)
        so = _bmm1(jnp.concatenate([q_eff[sl], m_mat[sl]], axis=1), hs, "pij,pjk->pik")
        o_bd = so[:, :c2, :] + o_intra[sl]
        hs = so[:, c2:, :] + g_mat[sl]
        o_ck.append(jnp.where(lo, o_bd[:, :c, :], o_bd[:, c:, :]))
    h_ref[...] = hs

    rk = rk_ref[...]
    lnw = lnw_ref[...]
    lnb = lnb_ref[...]
    inv_n = 1.0 / RWKV_HEAD
    for t in range(nt):
        sl = slice(t * LANES, (t + 1) * LANES)
        ot = jnp.concatenate([o_ck[ci][t] for ci in range(nck)], axis=0)
        s0, s1 = _seg2(ot)
        cen = ot - jnp.where(lo2, s0, s1) * inv_n
        q0, q1 = _seg2(cen * cen)
        on = cen * lax.rsqrt(jnp.where(lo2, q0, q1) * inv_n + GN_EPS) * lnw[:, sl] + lnb[:, sl]
        b0, b1 = _seg2(r[:, sl] * k[:, sl] * rk[:, sl])
        on = on + jnp.where(lo2, b0, b1) * v[:, sl]
        o_ref[:, sl] = (on * g[:, sl]).astype(o_ref.dtype)


def _rwkv(x, gmix, w, mu, w0, ww, a0, wa, wg, kkp, kap, rk, lnw, lnb, batch, seq, tb):
    n, d = x.shape
    spb = seq // tb
    res = lambda a: pl.BlockSpec(a.shape, lambda bi, si: (0,) * a.ndim, pipeline_mode=pl.Buffered(1))
    return pl.pallas_call(
        _rwkv_kernel,
        grid=(batch, spb),
        in_specs=[pl.BlockSpec((tb, d), lambda bi, si: (bi * spb + si, 0))]
        + [res(a) for a in (gmix, w, mu, w0, ww, a0, wa, wg, kkp, kap, rk, lnw, lnb)],
        out_specs=pl.BlockSpec((tb, RWKV_DIM), lambda bi, si: (bi * spb + si, 0)),
        out_shape=jax.ShapeDtypeStruct((n, RWKV_DIM), BF16),
        scratch_shapes=[pltpu.VMEM((RWKV_DIM // LANES, LANES, LANES), F32), pltpu.VMEM((1, w.shape[1]), F32)],
        compiler_params=_cparams(("parallel", "arbitrary")),
        name="rwkv",
    )(x, gmix, w, mu, w0, ww, a0, wa, wg, kkp, kap, rk, lnw, lnb)


def _resident(a):
    return pl.BlockSpec(a.shape, lambda i: (0,) * a.ndim, pipeline_mode=pl.Buffered(1))


def _out_proj_kernel(x_ref, oa_ref, ob_ref, w_ref, o_ref, w16_ref):
    @pl.when(pl.program_id(0) == 0)
    def _():
        w16_ref[...] = w_ref[...].astype(BF16)

    ka = oa_ref.shape[1]
    o_ref[...] = x_ref[...] + _dot(oa_ref[...], w16_ref[:ka, :]) + _dot(ob_ref[...], w16_ref[ka:, :])


def _out_proj(x, oa, ob, w, tm):
    n, d = x.shape
    row_blk = lambda a: pl.BlockSpec((tm, a.shape[1]), lambda i: (i, 0))
    return pl.pallas_call(
        _out_proj_kernel,
        grid=(n // tm,),
        in_specs=[row_blk(x), row_blk(oa), row_blk(ob), _resident(w)],
        out_specs=row_blk(x),
        out_shape=jax.ShapeDtypeStruct((n, d), F32),
        scratch_shapes=[pltpu.VMEM(w.shape, BF16)],
        compiler_params=_cparams(("arbitrary",)),
        name="out_proj",
    )(x, oa, ob, w)


def _ple_kernel(x_ref, g_ref, p_ref, wg_ref, wp_ref, o_ref, wg16_ref, wp16_ref):
    @pl.when(pl.program_id(0) == 0)
    def _():
        wg16_ref[...] = wg_ref[...].astype(BF16)
        wp16_ref[...] = wp_ref[...].astype(BF16)

    x = x_ref[...]
    gate = jax.nn.sigmoid(_dot(_rms(x, g_ref[...]).astype(BF16), wg16_ref[...]))
    o_ref[...] = x + gate * _dot(p_ref[...].astype(BF16), wp16_ref[...])


def _ple(x, g, p, wg, wp, tm):
    n, d = x.shape
    row_blk = lambda a: pl.BlockSpec((tm, a.shape[1]), lambda i: (i, 0))
    return pl.pallas_call(
        _ple_kernel,
        grid=(n // tm,),
        in_specs=[row_blk(x), _resident(g), row_blk(p), _resident(wg), _resident(wp)],
        out_specs=row_blk(x),
        out_shape=jax.ShapeDtypeStruct((n, d), F32),
        scratch_shapes=[pltpu.VMEM(wg.shape, BF16), pltpu.VMEM(wp.shape, BF16)],
        compiler_params=_cparams(("arbitrary",)),
        name="ple",
    )(x, g, p, wg, wp)


def _tile(n, pref):
    t = min(n, pref)
    assert n % t == 0, (n, t)
    return t


def _layer(x, p, norm_ffn1, w1_gate, w1_up, w1_down, norm_mix, w_in, q_a_norm, w_q_b, kv_a_norm, w_kv_b,
           q_norm, k_norm, mu_shift, w0, w_w2, a0, w_a2, w_g2, k_k, k_a, r_k, ln_x_w, ln_x_b, w_out,
           norm_ffn2, w2_gate, w2_up, w2_down, norm_ple, w_ple_gate, w_ple_proj):
    batch, seq, d = x.shape
    n = batch * seq
    row = lambda a: a.reshape(1, -1)
    bf = lambda a: a.astype(BF16)
    mla_in = Q_LORA + KV_LORA + QK_ROPE
    dff = w1_gate.shape[1]
    tf = _tile(dff, 256)
    tm_ffn = _tile(n, 1024)

    xf = x.reshape(n, d)
    x1 = _ffn(xf, row(norm_ffn1), w1_gate, w1_up, w1_down, tm_ffn, tf)

    w_mla = bf(jnp.concatenate([w_in[:, :mla_in], w_in[:, mla_in - QK_ROPE:mla_in]], axis=1))
    wq = w_q_b.reshape(Q_LORA, MLA_HEADS, QK_HEAD)
    wqb = bf(jnp.concatenate([wq[:, :, :QK_NOPE].reshape(Q_LORA, -1), wq[:, :, QK_NOPE:].reshape(Q_LORA, -1)], axis=1))
    inv_freq = 1.0 / (ROPE_BASE ** (jnp.arange(0, QK_ROPE, 2, dtype=F32) / QK_ROPE))
    ang = jnp.arange(seq, dtype=F32)[:, None] * inv_freq[None, :]
    cos, sin = jnp.cos(ang), jnp.sin(ang)
    cos_t = jnp.concatenate([cos, cos, cos, cos], axis=1)
    sin_t = jnp.concatenate([-sin, sin, -sin, sin], axis=1)
    pair = lambda a: row(jnp.concatenate([a, a]))
    tm_prep = _tile(seq, 256)
    q, k, v = _mla_prep(x1, row(norm_mix), w_mla, row(q_a_norm), wqb, row(kv_a_norm), bf(w_kv_b),
                        row(q_norm[:QK_NOPE]), pair(q_norm[QK_NOPE:]), row(k_norm[:QK_NOPE]),
                        pair(k_norm[QK_NOPE:]), cos_t, sin_t, batch, seq, tm_prep)
    o_mla = _flash(q, k, v, _tile(seq, 1024), _tile(seq, 512)).reshape(n, MLA_HEADS * V_HEAD)

    zero = jnp.zeros_like(w_w2)
    ww = bf(jnp.concatenate([w_w2, zero], axis=0))
    wa = bf(jnp.concatenate([zero, w_a2], axis=0))
    o_rwkv = _rwkv(x1, row(norm_mix), bf(w_in[:, mla_in:]), row(mu_shift), row(w0), ww, row(a0), wa, bf(w_g2),
                   row(k_k), row(k_a), row(r_k), row(ln_x_w), row(ln_x_b), batch, seq, _tile(seq, 2 * CHUNK))

    x2 = _out_proj(x1, o_mla, o_rwkv, w_out, _tile(n, 512))
    x3 = _ffn(x2, row(norm_ffn2), w2_gate, w2_up, w2_down, tm_ffn, tf)
    out = _ple(x3, row(norm_ple), p.reshape(n, -1), w_ple_gate, w_ple_proj, _tile(n, 256))
    return out.reshape(batch, seq, d)


def kernel(x, p, norm_ffn1, w1_gate, w1_up, w1_down, norm_mix, w_in, q_a_norm, w_q_b, kv_a_norm, w_kv_b, q_norm, k_norm, mu_shift, w0, w_w2, a0, w_a2, w_g2, k_k, k_a, r_k, ln_x_w, ln_x_b, w_out, norm_ffn2, w2_gate, w2_up, w2_down, norm_ple, w_ple_gate, w_ple_proj):
    depth = p.shape[0]
    for i in range(depth):
        x = _layer(x, p[i], norm_ffn1[i], w1_gate[i], w1_up[i], w1_down[i], norm_mix[i], w_in[i], q_a_norm[i],
                   w_q_b[i], kv_a_norm[i], w_kv_b[i], q_norm[i], k_norm[i], mu_shift[i], w0[i], w_w2[i], a0[i],
                   w_a2[i], w_g2[i], k_k[i], k_a[i], r_k[i], ln_x_w[i], ln_x_b[i], w_out[i], norm_ffn2[i],
                   w2_gate[i], w2_up[i], w2_down[i], norm_ple[i], w_ple_gate[i], w_ple_proj[i])
    return x
```

```python
import functools
import math

import jax
import jax.numpy as jnp
from jax import lax
from jax.experimental import pallas as pl
from jax.experimental.pallas import tpu as pltpu

F32 = jnp.float32
BF16 = jnp.bfloat16

EPS = 1e-6
GN_EPS = 64e-5
MLA_HEADS = 8
QK_NOPE = 128
QK_ROPE = 64
QK_HEAD = QK_NOPE + QK_ROPE
V_HEAD = 128
Q_LORA = 512
KV_LORA = 256
ROPE_BASE = 10000.0
RWKV_HEAD = 64
RWKV_HEADS = 16
RWKV_DIM = RWKV_HEADS * RWKV_HEAD
LANES = 128
MXU_DIM = 256
CHUNK = 64
VMEM_LIMIT = 56 * 1024 * 1024
SCAN_PASSES = (1, 1, 1, 1, 1, 1)


def _cparams(sem):
    return pltpu.CompilerParams(dimension_semantics=sem, vmem_limit_bytes=VMEM_LIMIT)


def _dot(a, b):
    return jnp.dot(a, b, preferred_element_type=F32)


def _rms(x, g):
    return x * lax.rsqrt(jnp.mean(x * x, axis=-1, keepdims=True) + EPS) * g


def _ffn_kernel(x_ref, g_ref, wg_ref, wu_ref, wd_ref, o_ref, h_ref):
    j = pl.program_id(1)

    @pl.when(j == 0)
    def _():
        h_ref[...] = _rms(x_ref[...], g_ref[...]).astype(BF16)
        o_ref[...] = jnp.zeros_like(o_ref)

    h = h_ref[...]
    gate = _dot(h, wg_ref[...].astype(BF16))
    up = _dot(h, wu_ref[...].astype(BF16))
    act = (gate * jax.nn.sigmoid(gate) * up).astype(BF16)
    o_ref[...] += _dot(act, wd_ref[...].astype(BF16))

    @pl.when(j == pl.num_programs(1) - 1)
    def _():
        o_ref[...] = x_ref[...] + 0.5 * o_ref[...]


def _ffn(x, g, wg, wu, wd, tm, tf):
    n, d = x.shape
    dff = wg.shape[1]
    return pl.pallas_call(
        _ffn_kernel,
        grid=(n // tm, dff // tf),
        in_specs=[
            pl.BlockSpec((tm, d), lambda i, j: (i, 0)),
            pl.BlockSpec((1, d), lambda i, j: (0, 0)),
            pl.BlockSpec((d, tf), lambda i, j: (0, j)),
            pl.BlockSpec((d, tf), lambda i, j: (0, j)),
            pl.BlockSpec((tf, d), lambda i, j: (j, 0)),
        ],
        out_specs=pl.BlockSpec((tm, d), lambda i, j: (i, 0)),
        out_shape=jax.ShapeDtypeStruct((n, d), F32),
        scratch_shapes=[pltpu.VMEM((tm, d), BF16)],
        compiler_params=_cparams(("parallel", "arbitrary")),
        name="ffn",
    )(x, g, wg, wu, wd)


def _norm_mm_kernel(x_ref, g_ref, w_ref, o_ref, h_ref):
    @pl.when(pl.program_id(1) == 0)
    def _():
        h_ref[...] = _rms(x_ref[...], g_ref[...]).astype(BF16)

    o_ref[...] = _dot(h_ref[...], w_ref[...])


def _norm_mm(x, g, w, tm, tn):
    n, d = x.shape
    nc = w.shape[1]
    return pl.pallas_call(
        _norm_mm_kernel,
        grid=(n // tm, nc // tn),
        in_specs=[
            pl.BlockSpec((tm, d), lambda i, j: (i, 0)),
            pl.BlockSpec((1, d), lambda i, j: (0, 0)),
            pl.BlockSpec((d, tn), lambda i, j: (0, j)),
        ],
        out_specs=pl.BlockSpec((tm, tn), lambda i, j: (i, j)),
        out_shape=jax.ShapeDtypeStruct((n, nc), F32),
        scratch_shapes=[pltpu.VMEM((tm, d), BF16)],
        compiler_params=_cparams(("parallel", "arbitrary")),
        name="norm_mm",
    )(x, g, w)


def _seg2(x):
    lo = lax.broadcasted_iota(jnp.int32, x.shape, 1) < RWKV_HEAD
    s0 = jnp.sum(jnp.where(lo, x, 0.0), axis=-1, keepdims=True)
    s1 = jnp.sum(jnp.where(lo, 0.0, x), axis=-1, keepdims=True)
    return s0, s1


def _rope_pair(y, cos_t, sin_t):
    lane = lax.broadcasted_iota(jnp.int32, y.shape, 1)
    first = (lane % QK_ROPE) < (QK_ROPE // 2)
    rot = jnp.where(first, pltpu.roll(y, LANES - QK_ROPE // 2, axis=1), pltpu.roll(y, QK_ROPE // 2, axis=1))
    return y * cos_t + rot * sin_t


def _mla_prep_kernel(x_ref, gmix_ref, win_ref, qag_ref, wqb_ref, kvag_ref, wkvb_ref,
                     gqn_ref, gqp_ref, gkn_ref, gkp_ref, cos_ref, sin_ref,
                     q_ref, k_ref, v_ref):
    scale = 1.0 / math.sqrt(QK_HEAD)
    h = _rms(x_ref[...], gmix_ref[...]).astype(BF16)
    z = _dot(h, win_ref[...])
    q_lat = z[:, :Q_LORA]
    kv_lat = z[:, Q_LORA:Q_LORA + KV_LORA]
    kpe2 = z[:, Q_LORA + KV_LORA:]
    qf = _dot(_rms(q_lat, qag_ref[...]).astype(BF16), wqb_ref[...])
    kvf = _dot(_rms(kv_lat, kvag_ref[...]).astype(BF16), wkvb_ref[...])
    cos_t = cos_ref[...]
    sin_t = sin_ref[...]
    lo = lax.broadcasted_iota(jnp.int32, cos_t.shape, 1) < QK_ROPE
    kpe_ss, _ = _seg2(kpe2 * kpe2)
    nope_w = MLA_HEADS * QK_NOPE
    for j in range(MLA_HEADS // 2):
        qp = qf[:, nope_w + j * LANES: nope_w + (j + 1) * LANES]
        qs0, qs1 = _seg2(qp * qp)
        rs_q = []
        rs_k = []
        for e, qs in ((0, qs0), (1, qs1)):
            hd = 2 * j + e
            qn = qf[:, hd * QK_NOPE:(hd + 1) * QK_NOPE]
            rq = lax.rsqrt((jnp.sum(qn * qn, axis=-1, keepdims=True) + qs) * (1.0 / QK_HEAD) + EPS)
            q_ref[0, hd, :, :QK_NOPE] = (qn * rq * gqn_ref[...] * scale).astype(BF16)
            rs_q.append(rq)
            kn = kvf[:, hd * 2 * QK_NOPE: hd * 2 * QK_NOPE + QK_NOPE]
            rk = lax.rsqrt((jnp.sum(kn * kn, axis=-1, keepdims=True) + kpe_ss) * (1.0 / QK_HEAD) + EPS)
            k_ref[0, hd, :, :QK_NOPE] = (kn * rk * gkn_ref[...]).astype(BF16)
            rs_k.append(rk)
            v_ref[0, hd] = kvf[:, hd * 2 * QK_NOPE + QK_NOPE:(hd + 1) * 2 * QK_NOPE].astype(BF16)
        yq = _rope_pair(qp * jnp.where(lo, rs_q[0], rs_q[1]) * gqp_ref[...], cos_t, sin_t) * scale
        yk = _rope_pair(kpe2 * jnp.where(lo, rs_k[0], rs_k[1]) * gkp_ref[...], cos_t, sin_t)
        q_ref[0, 2 * j, :, QK_NOPE:] = jnp.where(lo, yq, 0.0).astype(BF16)
        q_ref[0, 2 * j + 1, :, QK_NOPE:] = jnp.where(lo, 0.0, yq).astype(BF16)
        k_ref[0, 2 * j, :, QK_NOPE:] = jnp.where(lo, yk, 0.0).astype(BF16)
        k_ref[0, 2 * j + 1, :, QK_NOPE:] = jnp.where(lo, 0.0, yk).astype(BF16)


def _mla_prep(x, gmix, win, qag, wqb, kvag, wkvb, gqn, gqp, gkn, gkp, cos_t, sin_t, batch, seq, tm):
    n, d = x.shape
    spt = seq // tm
    full = lambda a: pl.BlockSpec(a.shape, lambda i: (0,) * a.ndim)
    qk_w = 2 * LANES
    return pl.pallas_call(
        _mla_prep_kernel,
        grid=(n // tm,),
        in_specs=[
            pl.BlockSpec((tm, d), lambda i: (i, 0)),
            full(gmix), full(win), full(qag), full(wqb), full(kvag), full(wkvb),
            full(gqn), full(gqp), full(gkn), full(gkp),
            pl.BlockSpec((tm, LANES), lambda i: (i % spt, 0)),
            pl.BlockSpec((tm, LANES), lambda i: (i % spt, 0)),
        ],
        out_specs=[
            pl.BlockSpec((1, MLA_HEADS, tm, qk_w), lambda i: (i // spt, 0, i % spt, 0)),
            pl.BlockSpec((1, MLA_HEADS, tm, qk_w), lambda i: (i // spt, 0, i % spt, 0)),
            pl.BlockSpec((1, MLA_HEADS, tm, V_HEAD), lambda i: (i // spt, 0, i % spt, 0)),
        ],
        out_shape=[
            jax.ShapeDtypeStruct((batch, MLA_HEADS, seq, qk_w), BF16),
            jax.ShapeDtypeStruct((batch, MLA_HEADS, seq, qk_w), BF16),
            jax.ShapeDtypeStruct((batch, MLA_HEADS, seq, V_HEAD), BF16),
        ],
        compiler_params=_cparams(("parallel",)),
        name="mla_prep",
    )(x, gmix, win, qag, wqb, kvag, wkvb, gqn, gqp, gkn, gkp, cos_t, sin_t)


def _flash_kernel(q_ref, k_ref, v_ref, o_ref, m_ref, l_ref, acc_ref, *, tk, nsub):
    qi = pl.program_id(2)
    m_ref[...] = jnp.full_like(m_ref, -1e30)
    l_ref[...] = jnp.zeros_like(l_ref)
    acc_ref[...] = jnp.zeros_like(acc_ref)
    rep = tk // LANES
    causal = lax.broadcasted_iota(jnp.int32, (tk, tk), 1) <= lax.broadcasted_iota(jnp.int32, (tk, tk), 0)

    def step(sub, j, diag):
        rows = pl.ds(sub * tk, tk)
        kv = pl.ds(pl.multiple_of(j * tk, tk), tk)
        s = lax.dot_general(q_ref[0, 0, rows, :], k_ref[0, 0, kv, :], (((1,), (1,)), ((), ())),
                            preferred_element_type=F32)
        if diag:
            s = jnp.where(causal, s, -1e30)
        m_prev = m_ref[rows, :]
        m_new = jnp.maximum(m_prev, jnp.max(s, axis=-1, keepdims=True))
        alpha = jnp.exp(m_prev - m_new)
        p = jnp.exp(s - jnp.concatenate([m_new] * rep, axis=1))
        l_ref[rows, :] = alpha * l_ref[rows, :] + jnp.sum(p, axis=-1, keepdims=True)
        acc_ref[rows, :] = alpha * acc_ref[rows, :] + _dot(p.astype(BF16), v_ref[0, 0, kv, :])
        m_ref[rows, :] = m_new

    def body(j, carry):
        for sub in range(nsub):
            step(sub, j, False)
        return carry

    lax.fori_loop(0, qi * nsub, body, 0)
    for e in range(nsub):
        for sub in range(e, nsub):
            step(sub, qi * nsub + e, sub == e)
    o_ref[0] = (acc_ref[...] / l_ref[...]).astype(o_ref.dtype)


def _flash(q, k, v, tq, tk):
    b, hh, s, dk = q.shape
    dv = v.shape[-1]
    assert dv == LANES and tq % tk == 0
    return pl.pallas_call(
        functools.partial(_flash_kernel, tk=tk, nsub=tq // tk),
        grid=(b, hh, s // tq),
        in_specs=[
            pl.BlockSpec((1, 1, tq, dk), lambda bi, h, qi: (bi, h, qi, 0)),
            pl.BlockSpec((1, 1, s, dk), lambda bi, h, qi: (bi, h, 0, 0)),
            pl.BlockSpec((1, 1, s, dv), lambda bi, h, qi: (bi, h, 0, 0)),
        ],
        out_specs=pl.BlockSpec((1, tq, dv), lambda bi, h, qi: (bi, qi, h)),
        out_shape=jax.ShapeDtypeStruct((b, s, hh * dv), BF16),
        scratch_shapes=[pltpu.VMEM((tq, LANES), F32), pltpu.VMEM((tq, LANES), F32), pltpu.VMEM((tq, dv), F32)],
        compiler_params=_cparams(("parallel", "parallel", "arbitrary")),
        name="flash",
    )(q, k, v)


def _rwkv_prep_kernel(z_ref, zp_ref, mu_ref, w0_ref, ww_ref, a0_ref, wa_ref, wg_ref, kk_ref, ka_ref,
                      r_o, k_o, v_o, kk_o, a_o, lw_o, g_o, *, seq):
    tm = z_ref.shape[0]
    z = z_ref[...]
    first = (pl.program_id(0) * tm) % seq == 0
    prow = jnp.where(first, 0.0, zp_ref[7:8, :])
    rowid = lax.broadcasted_iota(jnp.int32, z.shape, 0)
    prev = jnp.where(rowid == 0, prow, pltpu.roll(z, 1, axis=0))
    zs = z + (prev - z) * mu_ref[...]
    d = RWKV_DIM
    r = zs[:, :d]
    k = zs[:, d:2 * d]
    v = zs[:, 2 * d:3 * d]
    wa_lo = zs[:, 3 * d:3 * d + LANES]
    g_lo = zs[:, 3 * d + LANES:]
    wpre = w0_ref[...] + _dot(jnp.tanh(wa_lo).astype(BF16), ww_ref[...])
    y = -wpre
    w = -(jnp.maximum(y, 0.0) + jnp.log(1.0 + jnp.exp(-jnp.abs(y)))) - 0.5
    a = jax.nn.sigmoid(a0_ref[...] + _dot(wa_lo.astype(BF16), wa_ref[...]))
    g = _dot(jax.nn.sigmoid(g_lo).astype(BF16), wg_ref[...])
    r_o[...] = r
    v_o[...] = v
    a_o[...] = a
    g_o[...] = g
    lw_o[...] = -jnp.exp(w)
    k_o[...] = k * (1.0 + (a - 1.0) * ka_ref[...])
    kk = k * kk_ref[...]
    lo = lax.broadcasted_iota(jnp.int32, (tm, LANES), 1) < RWKV_HEAD
    for t in range(d // LANES):
        kt = kk[:, t * LANES:(t + 1) * LANES]
        s0, s1 = _seg2(kt * kt)
        nrm = jnp.where(lo, jnp.maximum(jnp.sqrt(s0), 1e-12), jnp.maximum(jnp.sqrt(s1), 1e-12))
        kk_o[:, t * LANES:(t + 1) * LANES] = kt / nrm


def _rwkv_prep(z, mu, w0, ww, a0, wa, wg, kkp, kap, seq, tm):
    n, zin = z.shape
    d = RWKV_DIM
    full = lambda a: pl.BlockSpec(a.shape, lambda i: (0,) * a.ndim)
    out = pl.BlockSpec((tm, d), lambda i: (i, 0))
    return pl.pallas_call(
        functools.partial(_rwkv_prep_kernel, seq=seq),
        grid=(n // tm,),
        in_specs=[
            pl.BlockSpec((tm, zin), lambda i: (i, 0)),
            pl.BlockSpec((8, zin), lambda i: (jnp.maximum(i * (tm // 8) - 1, 0), 0)),
            full(mu), full(w0), full(ww), full(a0), full(wa), full(wg), full(kkp), full(kap),
        ],
        out_specs=[out] * 7,
        out_shape=[jax.ShapeDtypeStruct((n, d), F32)] * 7,
        compiler_params=_cparams(("parallel",)),
        name="rwkv_prep",
    )(z, z, mu, w0, ww, a0, wa, wg, kkp, kap)


def _split(x):
    hi = x.astype(BF16)
    return hi, (x - hi.astype(F32)).astype(BF16)


def _bmm(a, b, spec, passes):
    mm = lambda u, w: jnp.einsum(spec, u, w, preferred_element_type=F32)
    if passes == 1:
        return mm(a.astype(BF16), b.astype(BF16))
    ah, al = _split(a)
    bh, bl = _split(b)
    return mm(ah, bh) + mm(al, bh) + mm(ah, bl)


def _tiles(x):
    return jnp.stack([x[:, t * LANES:(t + 1) * LANES] for t in range(x.shape[1] // LANES)])


def _rwkv_scan_kernel(r_ref, k_ref, v_ref, kk_ref, a_ref, lw_ref, g_ref, rk_ref, lnw_ref, lnb_ref,
                      o_ref, h_ref, *, passes):
    c = CHUNK
    c2 = 2 * c

    @pl.when(pl.program_id(1) == 0)
    def _():
        h_ref[...] = jnp.zeros_like(h_ref)

    lw = lw_ref[...]
    ti = lax.broadcasted_iota(jnp.int32, (c, c), 0)
    si = lax.broadcasted_iota(jnp.int32, (c, c), 1)
    tril = jnp.where(si <= ti, 1.0, 0.0)
    lwh, lwl = _split(lw)
    lwl2 = (lw - lwh.astype(F32) - lwl.astype(F32)).astype(BF16)
    trilb = tril.astype(BF16)
    cum = _dot(trilb, lwh) + _dot(trilb, lwl) + _dot(trilb, lwl2)
    p = jnp.exp(cum)
    pinv = jnp.exp(-cum)
    pprev = jnp.exp(cum - lw)
    pend = jnp.exp(cum[c - 1:c, :] - cum)
    r = r_ref[...]
    k = k_ref[...]
    v = v_ref[...]
    kk = kk_ref[...]
    b = kk * a_ref[...]
    at = _tiles(-kk * pprev)
    bt = _tiles(b * pinv)
    kt = _tiles(k * pinv)
    rt = _tiles(r * p)
    bh = _tiles(b * pend)
    kh = _tiles(k * pend)
    vt = _tiles(v)
    pc = _tiles(jnp.exp(cum[c - 1:c, :]))

    lo = lax.broadcasted_iota(jnp.int32, (1, 1, LANES), 2) < RWKV_HEAD

    def bd(x):
        return jnp.concatenate([jnp.where(lo, x, 0.0), jnp.where(lo, 0.0, x)], axis=1)

    a_bd, b_bd, k_bd, r_bd, v_bd = bd(at), bd(bt), bd(kt), bd(rt), bd(vt)
    bh_bd, kh_bd = bd(bh), bd(kh)

    amat = _bmm(jnp.concatenate([a_bd, r_bd], axis=1), jnp.concatenate([b_bd, k_bd], axis=1),
                "pil,pjl->pij", passes[0])
    ri = lax.broadcasted_iota(jnp.int32, (1, c2, c2), 1)
    ci = lax.broadcasted_iota(jnp.int32, (1, c2, c2), 2)
    same = (ri // c) == (ci // c)
    strict = same & ((ri % c) > (ci % c))
    incl = same & ((ri % c) >= (ci % c))
    a_ab = jnp.where(strict, amat[:, :c2, :c2], 0.0)
    a_ak = jnp.where(strict, amat[:, :c2, c2:], 0.0)
    a_rb = jnp.where(incl, amat[:, c2:, :c2], 0.0)
    a_rk = jnp.where(incl, amat[:, c2:, c2:], 0.0)

    eye = jnp.where(ri == ci, 1.0, 0.0)
    tinv = eye + a_ab
    apow = a_ab
    for _ in range(int(math.log2(c)) - 1):
        apow = _bmm(apow, apow, "pij,pjk->pik", passes[1])
        tinv = tinv + _bmm(tinv, apow, "pij,pjk->pik", passes[1])

    akv = _bmm(a_ak, v_bd, "pij,pjk->pik", passes[2])
    wu = _bmm(tinv, jnp.concatenate([a_bd, akv], axis=2), "pij,pjk->pik", passes[3])
    rhs = jnp.concatenate([wu, jnp.concatenate([jnp.zeros_like(v_bd), v_bd], axis=2)], axis=1)
    top = _bmm(jnp.concatenate([a_rb, a_rk], axis=2), rhs, "pij,pjk->pik", passes[4])
    bot = _bmm(jnp.concatenate([bh_bd, kh_bd], axis=1), rhs, "psk,psn->pkn", passes[4])
    q_eff = r_bd + top[:, :, :c2]
    o_intra = top[:, :, c2:]
    m_mat = eye * pc + bot[:, :, :c2]
    g_mat = bot[:, :, c2:]

    h0 = h_ref[...]
    seq_out = _bmm(jnp.concatenate([q_eff, m_mat], axis=1), h0, "pij,pjk->pik", passes[5])
    o_bd = seq_out[:, :c2, :] + o_intra
    h_ref[...] = seq_out[:, c2:, :] + g_mat
    o = jnp.where(lo, o_bd[:, :c, :], o_bd[:, c:, :])

    rk = rk_ref[...]
    lnw = lnw_ref[...]
    lnb = lnb_ref[...]
    g = g_ref[...]
    lo2 = lax.broadcasted_iota(jnp.int32, (c, LANES), 1) < RWKV_HEAD
    inv_n = 1.0 / RWKV_HEAD
    for t in range(RWKV_DIM // LANES):
        sl = slice(t * LANES, (t + 1) * LANES)
        ot = o[t]
        s0, s1 = _seg2(ot)
        cen = ot - jnp.where(lo2, s0, s1) * inv_n
        q0, q1 = _seg2(cen * cen)
        on = cen * lax.rsqrt(jnp.where(lo2, q0, q1) * inv_n + GN_EPS) * lnw[:, sl] + lnb[:, sl]
        b0, b1 = _seg2(r[:, sl] * k[:, sl] * rk[:, sl])
        on = on + jnp.where(lo2, b0, b1) * v[:, sl]
        o_ref[:, sl] = (on * g[:, sl]).astype(o_ref.dtype)


def _rwkv_scan(r, k, v, kk, a, lw, g, rk, lnw, lnb, batch, seq, passes):
    n, d = r.shape
    nck = seq // CHUNK
    blk = pl.BlockSpec((CHUNK, d), lambda bi, ci: (bi * nck + ci, 0))
    full = lambda x: pl.BlockSpec(x.shape, lambda bi, ci: (0,) * x.ndim)
    return pl.pallas_call(
        functools.partial(_rwkv_scan_kernel, passes=passes),
        grid=(batch, nck),
        in_specs=[blk] * 7 + [full(rk), full(lnw), full(lnb)],
        out_specs=blk,
        out_shape=jax.ShapeDtypeStruct((n, d), BF16),
        scratch_shapes=[pltpu.VMEM((d // LANES, LANES, LANES), F32)],
        compiler_params=_cparams(("parallel", "arbitrary")),
        name="rwkv_scan",
    )(r, k, v, kk, a, lw, g, rk, lnw, lnb)


def _bmm1(a, b, spec):
    return jnp.einsum(spec, a.astype(BF16), b.astype(BF16), preferred_element_type=F32)


def _rwkv_kernel(x_ref, gmix_ref, w_ref, mu_ref, w0_ref, ww_ref, a0_ref, wa_ref, wg_ref, kkp_ref, kap_ref,
                 rk_ref, lnw_ref, lnb_ref, o_ref, h_ref, zlast_ref):
    c = CHUNK
    c2 = 2 * c
    d = RWKV_DIM
    nt = d // LANES
    tb = x_ref.shape[0]
    nck = tb // c

    @pl.when(pl.program_id(1) == 0)
    def _():
        h_ref[...] = jnp.zeros_like(h_ref)
        zlast_ref[...] = jnp.zeros_like(zlast_ref)

    z = _dot(_rms(x_ref[...], gmix_ref[...]).astype(BF16), w_ref[...])
    rowid = lax.broadcasted_iota(jnp.int32, z.shape, 0)
    prev = jnp.where(rowid == 0, zlast_ref[...], pltpu.roll(z, 1, axis=0))
    zlast_ref[...] = z[tb - 1:tb, :]
    zs = z + (prev - z) * mu_ref[...]
    r = zs[:, :d]
    k0 = zs[:, d:2 * d]
    v = zs[:, 2 * d:3 * d]
    wa_lo = zs[:, 3 * d:3 * d + LANES]
    g_lo = zs[:, 3 * d + LANES:]
    y = -(w0_ref[...] + _dot(jnp.tanh(wa_lo).astype(BF16), ww_ref[...]))
    w = -(jnp.maximum(y, 0.0) + jnp.log(1.0 + jnp.exp(-jnp.abs(y)))) - 0.5
    lw = -jnp.exp(w)
    a = jax.nn.sigmoid(a0_ref[...] + _dot(wa_lo.astype(BF16), wa_ref[...]))
    g = _dot(jax.nn.sigmoid(g_lo).astype(BF16), wg_ref[...])
    k = k0 * (1.0 + (a - 1.0) * kap_ref[...])
    kk0 = k0 * kkp_ref[...]
    lo2 = lax.broadcasted_iota(jnp.int32, (tb, LANES), 1) < RWKV_HEAD
    kk_t = []
    for t in range(nt):
        kt = kk0[:, t * LANES:(t + 1) * LANES]
        s0, s1 = _seg2(kt * kt)
        kk_t.append(kt / jnp.where(lo2, jnp.maximum(jnp.sqrt(s0), 1e-12), jnp.maximum(jnp.sqrt(s1), 1e-12)))
    kk = jnp.concatenate(kk_t, axis=1)

    ti = lax.broadcasted_iota(jnp.int32, (tb, tb), 0)
    si = lax.broadcasted_iota(jnp.int32, (tb, tb), 1)
    tril = jnp.where((si <= ti) & (si // c == ti // c), 1.0, 0.0).astype(BF16)
    lw1 = lw.astype(BF16)
    lw2 = (lw - lw1.astype(F32)).astype(BF16)
    lw3 = (lw - lw1.astype(F32) - lw2.astype(F32)).astype(BF16)
    cum = _dot(tril, lw1) + _dot(tril, lw2) + _dot(tril, lw3)
    cend = jnp.concatenate([jnp.broadcast_to(cum[(ci + 1) * c - 1:(ci + 1) * c, :], (c, d)) for ci in range(nck)],
                           axis=0)
    p = jnp.exp(cum)
    pinv = jnp.exp(-cum)
    pprev = jnp.exp(cum - lw)
    pend = jnp.exp(cend - cum)
    b = kk * a

    qw = h_ref.shape[1]
    nq = d // qw
    hpq = qw // RWKV_HEAD
    nn = "pij,pjk->pik"

    def tiles(x):
        return jnp.stack([x[ci * c:(ci + 1) * c, q * qw:(q + 1) * qw]
                          for ci in range(nck) for q in range(nq)]).astype(BF16)

    lane_head = lax.broadcasted_iota(jnp.int32, (1, 1, qw), 2) // RWKV_HEAD

    def bdr(x):
        zero = jnp.zeros_like(x)
        return jnp.concatenate([jnp.where(lane_head == hd, x, zero) for hd in range(hpq)], axis=1)

    at, bt, kt, rt, vt = tiles(-kk * pprev), tiles(b * pinv), tiles(k * pinv), tiles(r * p), tiles(v)
    bht, kht = tiles(b * pend), tiles(k * pend)
    pc = jnp.stack([jnp.exp(cend[ci * c:ci * c + 1, q * qw:(q + 1) * qw])
                    for ci in range(nck) for q in range(nq)])

    amat = _bmm1(jnp.concatenate([at, rt], axis=1), jnp.concatenate([bdr(bt), bdr(kt)], axis=1), "pil,pjl->pij")
    trow = lax.broadcasted_iota(jnp.int32, (1, c, qw), 1)
    scol = lax.broadcasted_iota(jnp.int32, (1, c, qw), 2) % c
    strict = trow > scol
    incl = trow >= scol
    a_ab = jnp.where(strict, amat[:, :c, :qw], 0.0)
    a_ak = jnp.where(strict, amat[:, :c, qw:], 0.0)
    a_rb = jnp.where(incl, amat[:, c:, :qw], 0.0)
    a_rk = jnp.where(incl, amat[:, c:, qw:], 0.0)

    tinv = jnp.where(trow == scol, 1.0, 0.0) + a_ab
    apow = _bmm1(a_ab, bdr(a_ab.astype(BF16)), nn)
    levels = int(math.log2(c))
    for lvl in range(1, levels):
        pbd = bdr(apow.astype(BF16))
        if lvl < levels - 1:
            sq = _bmm1(jnp.concatenate([apow, tinv], axis=1), pbd, nn)
            apow, tinv = sq[:, :c], tinv + sq[:, c:]
        else:
            tinv = tinv + _bmm1(tinv, pbd, nn)

    akv = _bmm1(a_ak, bdr(vt), nn)
    wu = _bmm1(tinv, jnp.concatenate([bdr(at), bdr(akv.astype(BF16))], axis=2), nn)
    w_a = wu[:, :, :qw]
    u_v = wu[:, :, qw:]

    ht = h_ref[...]
    own = (lax.broadcasted_iota(jnp.int32, (1, qw, qw), 1) // RWKV_HEAD
           == lax.broadcasted_iota(jnp.int32, (1, qw, qw), 2) // RWKV_HEAD)
    o_ck = []
    for ci in range(nck):
        sl = slice(ci * nq, (ci + 1) * nq)
        s1 = _bmm1(jnp.concatenate([w_a[sl], rt[sl]], axis=1), ht, "pil,pjl->pij")
        u = (s1[:, :c] + u_v[sl]).astype(BF16)
        o_ck.append(s1[:, c:] + _bmm1(jnp.concatenate([a_rb[sl], a_rk[sl]], axis=2),
                                      jnp.concatenate([bdr(u), bdr(vt[sl])], axis=1), nn))
        upd = _bmm1(jnp.concatenate([u, vt[sl]], axis=1), jnp.concatenate([bht[sl], kht[sl]], axis=1),
                    "psv,psk->pvk")
        ht = ht * pc[sl] + jnp.where(own, upd, 0.0)
    h_ref[...] = ht

    rk = rk_ref[...]
    lnw = lnw_ref[...]
    lnb = lnb_ref[...]
    inv_n = 1.0 / RWKV_HEAD
    for t in range(nt):
        sl = slice(t * LANES, (t + 1) * LANES)
        qi, off = divmod(t * LANES, qw)
        ot = jnp.concatenate([o_ck[ci][qi][:, off:off + LANES] for ci in range(nck)], axis=0)
        s0, s1 = _seg2(ot)
        cen = ot - jnp.where(lo2, s0, s1) * inv_n
        q0, q1 = _seg2(cen * cen)
        on = cen * lax.rsqrt(jnp.where(lo2, q0, q1) * inv_n + GN_EPS) * lnw[:, sl] + lnb[:, sl]
        b0, b1 = _seg2(r[:, sl] * k[:, sl] * rk[:, sl])
        on = on + jnp.where(lo2, b0, b1) * v[:, sl]
        o_ref[:, sl] = (on * g[:, sl]).astype(o_ref.dtype)


def _rwkv(x, gmix, w, mu, w0, ww, a0, wa, wg, kkp, kap, rk, lnw, lnb, batch, seq, tb):
    n, d = x.shape
    spb = seq // tb
    res = lambda a: pl.BlockSpec(a.shape, lambda bi, si: (0,) * a.ndim, pipeline_mode=pl.Buffered(1))
    return pl.pallas_call(
        _rwkv_kernel,
        grid=(batch, spb),
        in_specs=[pl.BlockSpec((tb, d), lambda bi, si: (bi * spb + si, 0))]
        + [res(a) for a in (gmix, w, mu, w0, ww, a0, wa, wg, kkp, kap, rk, lnw, lnb)],
        out_specs=pl.BlockSpec((tb, RWKV_DIM), lambda bi, si: (bi * spb + si, 0)),
        out_shape=jax.ShapeDtypeStruct((n, RWKV_DIM), BF16),
        scratch_shapes=[pltpu.VMEM((RWKV_DIM // MXU_DIM, MXU_DIM, MXU_DIM), F32), pltpu.VMEM((1, w.shape[1]), F32)],
        compiler_params=_cparams(("parallel", "arbitrary")),
        name="rwkv",
    )(x, gmix, w, mu, w0, ww, a0, wa, wg, kkp, kap, rk, lnw, lnb)


def _resident(a):
    return pl.BlockSpec(a.shape, lambda i: (0,) * a.ndim, pipeline_mode=pl.Buffered(1))


def _out_proj_kernel(x_ref, oa_ref, ob_ref, w_ref, o_ref, w16_ref):
    @pl.when(pl.program_id(0) == 0)
    def _():
        w16_ref[...] = w_ref[...].astype(BF16)

    ka = oa_ref.shape[1]
    o_ref[...] = x_ref[...] + _dot(oa_ref[...], w16_ref[:ka, :]) + _dot(ob_ref[...], w16_ref[ka:, :])


def _out_proj(x, oa, ob, w, tm):
    n, d = x.shape
    row_blk = lambda a: pl.BlockSpec((tm, a.shape[1]), lambda i: (i, 0))
    return pl.pallas_call(
        _out_proj_kernel,
        grid=(n // tm,),
        in_specs=[row_blk(x), row_blk(oa), row_blk(ob), _resident(w)],
        out_specs=row_blk(x),
        out_shape=jax.ShapeDtypeStruct((n, d), F32),
        scratch_shapes=[pltpu.VMEM(w.shape, BF16)],
        compiler_params=_cparams(("arbitrary",)),
        name="out_proj",
    )(x, oa, ob, w)


def _ple_kernel(x_ref, g_ref, p_ref, wg_ref, wp_ref, o_ref, wg16_ref, wp16_ref):
    @pl.when(pl.program_id(0) == 0)
    def _():
        wg16_ref[...] = wg_ref[...].astype(BF16)
        wp16_ref[...] = wp_ref[...].astype(BF16)

    x = x_ref[...]
    gate = jax.nn.sigmoid(_dot(_rms(x, g_ref[...]).astype(BF16), wg16_ref[...]))
    o_ref[...] = x + gate * _dot(p_ref[...].astype(BF16), wp16_ref[...])


def _ple(x, g, p, wg, wp, tm):
    n, d = x.shape
    row_blk = lambda a: pl.BlockSpec((tm, a.shape[1]), lambda i: (i, 0))
    return pl.pallas_call(
        _ple_kernel,
        grid=(n // tm,),
        in_specs=[row_blk(x), _resident(g), row_blk(p), _resident(wg), _resident(wp)],
        out_specs=row_blk(x),
        out_shape=jax.ShapeDtypeStruct((n, d), F32),
        scratch_shapes=[pltpu.VMEM(wg.shape, BF16), pltpu.VMEM(wp.shape, BF16)],
        compiler_params=_cparams(("arbitrary",)),
        name="ple",
    )(x, g, p, wg, wp)


def _tile(n, pref):
    t = min(n, pref)
    assert n % t == 0, (n, t)
    return t


def _layer(x, p, norm_ffn1, w1_gate, w1_up, w1_down, norm_mix, w_in, q_a_norm, w_q_b, kv_a_norm, w_kv_b,
           q_norm, k_norm, mu_shift, w0, w_w2, a0, w_a2, w_g2, k_k, k_a, r_k, ln_x_w, ln_x_b, w_out,
           norm_ffn2, w2_gate, w2_up, w2_down, norm_ple, w_ple_gate, w_ple_proj):
    batch, seq, d = x.shape
    n = batch * seq
    row = lambda a: a.reshape(1, -1)
    bf = lambda a: a.astype(BF16)
    mla_in = Q_LORA + KV_LORA + QK_ROPE
    dff = w1_gate.shape[1]
    tf = _tile(dff, 256)
    tm_ffn = _tile(n, 1024)

    xf = x.reshape(n, d)
    x1 = _ffn(xf, row(norm_ffn1), w1_gate, w1_up, w1_down, tm_ffn, tf)

    w_mla = bf(jnp.concatenate([w_in[:, :mla_in], w_in[:, mla_in - QK_ROPE:mla_in]], axis=1))
    wq = w_q_b.reshape(Q_LORA, MLA_HEADS, QK_HEAD)
    wqb = bf(jnp.concatenate([wq[:, :, :QK_NOPE].reshape(Q_LORA, -1), wq[:, :, QK_NOPE:].reshape(Q_LORA, -1)], axis=1))
    inv_freq = 1.0 / (ROPE_BASE ** (jnp.arange(0, QK_ROPE, 2, dtype=F32) / QK_ROPE))
    ang = jnp.arange(seq, dtype=F32)[:, None] * inv_freq[None, :]
    cos, sin = jnp.cos(ang), jnp.sin(ang)
    cos_t = jnp.concatenate([cos, cos, cos, cos], axis=1)
    sin_t = jnp.concatenate([-sin, sin, -sin, sin], axis=1)
    pair = lambda a: row(jnp.concatenate([a, a]))
    tm_prep = _tile(seq, 256)
    q, k, v = _mla_prep(x1, row(norm_mix), w_mla, row(q_a_norm), wqb, row(kv_a_norm), bf(w_kv_b),
                        row(q_norm[:QK_NOPE]), pair(q_norm[QK_NOPE:]), row(k_norm[:QK_NOPE]),
                        pair(k_norm[QK_NOPE:]), cos_t, sin_t, batch, seq, tm_prep)
    o_mla = _flash(q, k, v, _tile(seq, 1024), _tile(seq, 512)).reshape(n, MLA_HEADS * V_HEAD)

    zero = jnp.zeros_like(w_w2)
    ww = bf(jnp.concatenate([w_w2, zero], axis=0))
    wa = bf(jnp.concatenate([zero, w_a2], axis=0))
    o_rwkv = _rwkv(x1, row(norm_mix), bf(w_in[:, mla_in:]), row(mu_shift), row(w0), ww, row(a0), wa, bf(w_g2),
                   row(k_k), row(k_a), row(r_k), row(ln_x_w), row(ln_x_b), batch, seq, _tile(seq, 4 * CHUNK))

    x2 = _out_proj(x1, o_mla, o_rwkv, w_out, _tile(n, 512))
    x3 = _ffn(x2, row(norm_ffn2), w2_gate, w2_up, w2_down, tm_ffn, tf)
    out = _ple(x3, row(norm_ple), p.reshape(n, -1), w_ple_gate, w_ple_proj, _tile(n, 256))
    return out.reshape(batch, seq, d)


def kernel(x, p, norm_ffn1, w1_gate, w1_up, w1_down, norm_mix, w_in, q_a_norm, w_q_b, kv_a_norm, w_kv_b, q_norm, k_norm, mu_shift, w0, w_w2, a0, w_a2, w_g2, k_k, k_a, r_k, ln_x_w, ln_x_b, w_out, norm_ffn2, w2_gate, w2_up, w2_down, norm_ple, w_ple_gate, w_ple_proj):
    depth = p.shape[0]
    for i in range(depth):
        x = _layer(x, p[i], norm_ffn1[i], w1_gate[i], w1_up[i], w1_down[i], norm_mix[i], w_in[i], q_a_norm[i],
                   w_q_b[i], kv_a_norm[i], w_kv_b[i], q_norm[i], k_norm[i], mu_shift[i], w0[i], w_w2[i], a0[i],
                   w_a2[i], w_g2[i], k_k[i], k_a[i], r_k[i], ln_x_w[i], ln_x_b[i], w_out[i], norm_ffn2[i],
                   w2_gate[i], w2_up[i], w2_down[i], norm_ple[i], w_ple_gate[i], w_ple_proj[i])
    return x
```

```python
import functools
import math

import jax
import jax.numpy as jnp
from jax import lax
from jax.experimental import pallas as pl
from jax.experimental.pallas import tpu as pltpu

F32 = jnp.float32
BF16 = jnp.bfloat16

EPS = 1e-6
GN_EPS = 64e-5
MLA_HEADS = 8
QK_NOPE = 128
QK_ROPE = 64
QK_HEAD = QK_NOPE + QK_ROPE
V_HEAD = 128
Q_LORA = 512
KV_LORA = 256
ROPE_BASE = 10000.0
RWKV_HEAD = 64
RWKV_HEADS = 16
RWKV_DIM = RWKV_HEADS * RWKV_HEAD
LANES = 128
MXU_DIM = 256
CHUNK = 64
VMEM_LIMIT = 56 * 1024 * 1024
SCAN_PASSES = (1, 1, 1, 1, 1, 1)


def _cparams(sem):
    return pltpu.CompilerParams(dimension_semantics=sem, vmem_limit_bytes=VMEM_LIMIT)


def _dot(a, b):
    return jnp.dot(a, b, preferred_element_type=F32)


def _rms(x, g):
    return x * lax.rsqrt(jnp.mean(x * x, axis=-1, keepdims=True) + EPS) * g


def _ffn_kernel(x_ref, g_ref, wg_ref, wu_ref, wd_ref, o_ref, h_ref):
    j = pl.program_id(1)

    @pl.when(j == 0)
    def _():
        h_ref[...] = _rms(x_ref[...], g_ref[...]).astype(BF16)
        o_ref[...] = jnp.zeros_like(o_ref)

    h = h_ref[...]
    gate = _dot(h, wg_ref[...].astype(BF16))
    up = _dot(h, wu_ref[...].astype(BF16))
    act = (gate * jax.nn.sigmoid(gate) * up).astype(BF16)
    o_ref[...] += _dot(act, wd_ref[...].astype(BF16))

    @pl.when(j == pl.num_programs(1) - 1)
    def _():
        o_ref[...] = x_ref[...] + 0.5 * o_ref[...]


def _ffn(x, g, wg, wu, wd, tm, tf):
    n, d = x.shape
    dff = wg.shape[1]
    return pl.pallas_call(
        _ffn_kernel,
        grid=(n // tm, dff // tf),
        in_specs=[
            pl.BlockSpec((tm, d), lambda i, j: (i, 0)),
            pl.BlockSpec((1, d), lambda i, j: (0, 0)),
            pl.BlockSpec((d, tf), lambda i, j: (0, j)),
            pl.BlockSpec((d, tf), lambda i, j: (0, j)),
            pl.BlockSpec((tf, d), lambda i, j: (j, 0)),
        ],
        out_specs=pl.BlockSpec((tm, d), lambda i, j: (i, 0)),
        out_shape=jax.ShapeDtypeStruct((n, d), F32),
        scratch_shapes=[pltpu.VMEM((tm, d), BF16)],
        compiler_params=_cparams(("parallel", "arbitrary")),
        name="ffn",
    )(x, g, wg, wu, wd)


def _norm_mm_kernel(x_ref, g_ref, w_ref, o_ref, h_ref):
    @pl.when(pl.program_id(1) == 0)
    def _():
        h_ref[...] = _rms(x_ref[...], g_ref[...]).astype(BF16)

    o_ref[...] = _dot(h_ref[...], w_ref[...])


def _norm_mm(x, g, w, tm, tn):
    n, d = x.shape
    nc = w.shape[1]
    return pl.pallas_call(
        _norm_mm_kernel,
        grid=(n // tm, nc // tn),
        in_specs=[
            pl.BlockSpec((tm, d), lambda i, j: (i, 0)),
            pl.BlockSpec((1, d), lambda i, j: (0, 0)),
            pl.BlockSpec((d, tn), lambda i, j: (0, j)),
        ],
        out_specs=pl.BlockSpec((tm, tn), lambda i, j: (i, j)),
        out_shape=jax.ShapeDtypeStruct((n, nc), F32),
        scratch_shapes=[pltpu.VMEM((tm, d), BF16)],
        compiler_params=_cparams(("parallel", "arbitrary")),
        name="norm_mm",
    )(x, g, w)


def _seg2(x):
    lo = lax.broadcasted_iota(jnp.int32, x.shape, 1) < RWKV_HEAD
    s0 = jnp.sum(jnp.where(lo, x, 0.0), axis=-1, keepdims=True)
    s1 = jnp.sum(jnp.where(lo, 0.0, x), axis=-1, keepdims=True)
    return s0, s1


def _rope_pair(y, cos_t, sin_t):
    lane = lax.broadcasted_iota(jnp.int32, y.shape, 1)
    first = (lane % QK_ROPE) < (QK_ROPE // 2)
    rot = jnp.where(first, pltpu.roll(y, LANES - QK_ROPE // 2, axis=1), pltpu.roll(y, QK_ROPE // 2, axis=1))
    return y * cos_t + rot * sin_t


def _mla_prep_kernel(x_ref, gmix_ref, win_ref, qag_ref, wqb_ref, kvag_ref, wkvb_ref,
                     gqn_ref, gqp_ref, gkn_ref, gkp_ref, cos_ref, sin_ref,
                     q_ref, k_ref, v_ref):
    scale = 1.0 / math.sqrt(QK_HEAD)
    nope_w = MLA_HEADS * QK_NOPE
    tm = x_ref.shape[0]
    nsplit = 2 if tm % 16 == 0 else 1
    rows_per = tm // nsplit
    for part in range(nsplit):
        rows = slice(part * rows_per, (part + 1) * rows_per)
        h = _rms(x_ref[rows, :], gmix_ref[...]).astype(BF16)
        z = _dot(h, win_ref[...])
        q_lat = z[:, :Q_LORA]
        kv_lat = z[:, Q_LORA:Q_LORA + KV_LORA]
        kpe2 = z[:, Q_LORA + KV_LORA:]
        qf = _dot(_rms(q_lat, qag_ref[...]).astype(BF16), wqb_ref[...])
        kvf = _dot(_rms(kv_lat, kvag_ref[...]).astype(BF16), wkvb_ref[...])
        cos_t = cos_ref[rows, :]
        sin_t = sin_ref[rows, :]
        lo = lax.broadcasted_iota(jnp.int32, cos_t.shape, 1) < QK_ROPE
        kpe_ss, _ = _seg2(kpe2 * kpe2)
        kpe_rot = _rope_pair(kpe2 * gkp_ref[...], cos_t, sin_t)
        for j in range(MLA_HEADS // 2):
            qp = qf[:, nope_w + j * LANES: nope_w + (j + 1) * LANES]
            qs0, qs1 = _seg2(qp * qp)
            rs_q = []
            rs_k = []
            for e, qs in ((0, qs0), (1, qs1)):
                hd = 2 * j + e
                qn = qf[:, hd * QK_NOPE:(hd + 1) * QK_NOPE]
                rq = lax.rsqrt((jnp.sum(qn * qn, axis=-1, keepdims=True) + qs) * (1.0 / QK_HEAD) + EPS)
                q_ref[0, hd, rows, :QK_NOPE] = (qn * rq * gqn_ref[...] * scale).astype(BF16)
                rs_q.append(rq)
                kn = kvf[:, hd * 2 * QK_NOPE: hd * 2 * QK_NOPE + QK_NOPE]
                rk = lax.rsqrt((jnp.sum(kn * kn, axis=-1, keepdims=True) + kpe_ss) * (1.0 / QK_HEAD) + EPS)
                k_ref[0, hd, rows, :QK_NOPE] = (kn * rk * gkn_ref[...]).astype(BF16)
                rs_k.append(rk)
                v_ref[0, hd, rows, :] = kvf[:, hd * 2 * QK_NOPE + QK_NOPE:(hd + 1) * 2 * QK_NOPE].astype(BF16)
            yq = _rope_pair(qp * jnp.where(lo, rs_q[0], rs_q[1]) * gqp_ref[...], cos_t, sin_t) * scale
            q_ref[0, 2 * j, rows, QK_NOPE:] = jnp.where(lo, yq, 0.0).astype(BF16)
            q_ref[0, 2 * j + 1, rows, QK_NOPE:] = jnp.where(lo, 0.0, yq).astype(BF16)
            k_ref[0, 2 * j, rows, QK_NOPE:] = jnp.where(lo, kpe_rot * rs_k[0], 0.0).astype(BF16)
            k_ref[0, 2 * j + 1, rows, QK_NOPE:] = jnp.where(lo, 0.0, kpe_rot * rs_k[1]).astype(BF16)


def _mla_prep(x, gmix, win, qag, wqb, kvag, wkvb, gqn, gqp, gkn, gkp, cos_t, sin_t, batch, seq, tm):
    n, d = x.shape
    spt = seq // tm
    full = lambda a: pl.BlockSpec(a.shape, lambda i: (0,) * a.ndim)
    qk_w = 2 * LANES
    return pl.pallas_call(
        _mla_prep_kernel,
        grid=(n // tm,),
        in_specs=[
            pl.BlockSpec((tm, d), lambda i: (i, 0)),
            full(gmix), full(win), full(qag), full(wqb), full(kvag), full(wkvb),
            full(gqn), full(gqp), full(gkn), full(gkp),
            pl.BlockSpec((tm, LANES), lambda i: (i % spt, 0)),
            pl.BlockSpec((tm, LANES), lambda i: (i % spt, 0)),
        ],
        out_specs=[
            pl.BlockSpec((1, MLA_HEADS, tm, qk_w), lambda i: (i // spt, 0, i % spt, 0)),
            pl.BlockSpec((1, MLA_HEADS, tm, qk_w), lambda i: (i // spt, 0, i % spt, 0)),
            pl.BlockSpec((1, MLA_HEADS, tm, V_HEAD), lambda i: (i // spt, 0, i % spt, 0)),
        ],
        out_shape=[
            jax.ShapeDtypeStruct((batch, MLA_HEADS, seq, qk_w), BF16),
            jax.ShapeDtypeStruct((batch, MLA_HEADS, seq, qk_w), BF16),
            jax.ShapeDtypeStruct((batch, MLA_HEADS, seq, V_HEAD), BF16),
        ],
        compiler_params=_cparams(("parallel",)),
        name="mla_prep",
    )(x, gmix, win, qag, wqb, kvag, wkvb, gqn, gqp, gkn, gkp, cos_t, sin_t)


def _flash_kernel(q_ref, k_ref, v_ref, o_ref, m_ref, l_ref, acc_ref, *, tk, nsub):
    qi = pl.program_id(2)
    m_ref[...] = jnp.full_like(m_ref, -1e30)
    l_ref[...] = jnp.zeros_like(l_ref)
    acc_ref[...] = jnp.zeros_like(acc_ref)
    rep = tk // LANES
    causal = lax.broadcasted_iota(jnp.int32, (tk, tk), 1) <= lax.broadcasted_iota(jnp.int32, (tk, tk), 0)

    def step(sub, j, diag):
        rows = pl.ds(sub * tk, tk)
        kv = pl.ds(pl.multiple_of(j * tk, tk), tk)
        s = lax.dot_general(q_ref[0, 0, rows, :], k_ref[0, 0, kv, :], (((1,), (1,)), ((), ())),
                            preferred_element_type=F32)
        if diag:
            s = jnp.where(causal, s, -1e30)
        m_prev = m_ref[rows, :]
        m_new = jnp.maximum(m_prev, jnp.max(s, axis=-1, keepdims=True))
        alpha = jnp.exp(m_prev - m_new)
        p = jnp.exp(s - jnp.concatenate([m_new] * rep, axis=1))
        l_ref[rows, :] = alpha * l_ref[rows, :] + jnp.sum(p, axis=-1, keepdims=True)
        acc_ref[rows, :] = alpha * acc_ref[rows, :] + _dot(p.astype(BF16), v_ref[0, 0, kv, :])
        m_ref[rows, :] = m_new

    def body(j, carry):
        for sub in range(nsub):
            step(sub, j, False)
        return carry

    lax.fori_loop(0, qi * nsub, body, 0)
    for e in range(nsub):
        for sub in range(e, nsub):
            step(sub, qi * nsub + e, sub == e)
    o_ref[0] = (acc_ref[...] / l_ref[...]).astype(o_ref.dtype)


def _flash(q, k, v, tq, tk):
    b, hh, s, dk = q.shape
    dv = v.shape[-1]
    assert dv == LANES and tq % tk == 0
    return pl.pallas_call(
        functools.partial(_flash_kernel, tk=tk, nsub=tq // tk),
        grid=(b, hh, s // tq),
        in_specs=[
            pl.BlockSpec((1, 1, tq, dk), lambda bi, h, qi: (bi, h, qi, 0)),
            pl.BlockSpec((1, 1, s, dk), lambda bi, h, qi: (bi, h, 0, 0)),
            pl.BlockSpec((1, 1, s, dv), lambda bi, h, qi: (bi, h, 0, 0)),
        ],
        out_specs=pl.BlockSpec((1, tq, dv), lambda bi, h, qi: (bi, qi, h)),
        out_shape=jax.ShapeDtypeStruct((b, s, hh * dv), BF16),
        scratch_shapes=[pltpu.VMEM((tq, LANES), F32), pltpu.VMEM((tq, LANES), F32), pltpu.VMEM((tq, dv), F32)],
        compiler_params=_cparams(("parallel", "parallel", "arbitrary")),
        name="flash",
    )(q, k, v)


def _rwkv_prep_kernel(z_ref, zp_ref, mu_ref, w0_ref, ww_ref, a0_ref, wa_ref, wg_ref, kk_ref, ka_ref,
                      r_o, k_o, v_o, kk_o, a_o, lw_o, g_o, *, seq):
    tm = z_ref.shape[0]
    z = z_ref[...]
    first = (pl.program_id(0) * tm) % seq == 0
    prow = jnp.where(first, 0.0, zp_ref[7:8, :])
    rowid = lax.broadcasted_iota(jnp.int32, z.shape, 0)
    prev = jnp.where(rowid == 0, prow, pltpu.roll(z, 1, axis=0))
    zs = z + (prev - z) * mu_ref[...]
    d = RWKV_DIM
    r = zs[:, :d]
    k = zs[:, d:2 * d]
    v = zs[:, 2 * d:3 * d]
    wa_lo = zs[:, 3 * d:3 * d + LANES]
    g_lo = zs[:, 3 * d + LANES:]
    wpre = w0_ref[...] + _dot(jnp.tanh(wa_lo).astype(BF16), ww_ref[...])
    y = -wpre
    w = -(jnp.maximum(y, 0.0) + jnp.log(1.0 + jnp.exp(-jnp.abs(y)))) - 0.5
    a = jax.nn.sigmoid(a0_ref[...] + _dot(wa_lo.astype(BF16), wa_ref[...]))
    g = _dot(jax.nn.sigmoid(g_lo).astype(BF16), wg_ref[...])
    r_o[...] = r
    v_o[...] = v
    a_o[...] = a
    g_o[...] = g
    lw_o[...] = -jnp.exp(w)
    k_o[...] = k * (1.0 + (a - 1.0) * ka_ref[...])
    kk = k * kk_ref[...]
    lo = lax.broadcasted_iota(jnp.int32, (tm, LANES), 1) < RWKV_HEAD
    for t in range(d // LANES):
        kt = kk[:, t * LANES:(t + 1) * LANES]
        s0, s1 = _seg2(kt * kt)
        nrm = jnp.where(lo, jnp.maximum(jnp.sqrt(s0), 1e-12), jnp.maximum(jnp.sqrt(s1), 1e-12))
        kk_o[:, t * LANES:(t + 1) * LANES] = kt / nrm


def _rwkv_prep(z, mu, w0, ww, a0, wa, wg, kkp, kap, seq, tm):
    n, zin = z.shape
    d = RWKV_DIM
    full = lambda a: pl.BlockSpec(a.shape, lambda i: (0,) * a.ndim)
    out = pl.BlockSpec((tm, d), lambda i: (i, 0))
    return pl.pallas_call(
        functools.partial(_rwkv_prep_kernel, seq=seq),
        grid=(n // tm,),
        in_specs=[
            pl.BlockSpec((tm, zin), lambda i: (i, 0)),
            pl.BlockSpec((8, zin), lambda i: (jnp.maximum(i * (tm // 8) - 1, 0), 0)),
            full(mu), full(w0), full(ww), full(a0), full(wa), full(wg), full(kkp), full(kap),
        ],
        out_specs=[out] * 7,
        out_shape=[jax.ShapeDtypeStruct((n, d), F32)] * 7,
        compiler_params=_cparams(("parallel",)),
        name="rwkv_prep",
    )(z, z, mu, w0, ww, a0, wa, wg, kkp, kap)


def _split(x):
    hi = x.astype(BF16)
    return hi, (x - hi.astype(F32)).astype(BF16)


def _bmm(a, b, spec, passes):
    mm = lambda u, w: jnp.einsum(spec, u, w, preferred_element_type=F32)
    if passes == 1:
        return mm(a.astype(BF16), b.astype(BF16))
    ah, al = _split(a)
    bh, bl = _split(b)
    return mm(ah, bh) + mm(al, bh) + mm(ah, bl)


def _tiles(x):
    return jnp.stack([x[:, t * LANES:(t + 1) * LANES] for t in range(x.shape[1] // LANES)])


def _rwkv_scan_kernel(r_ref, k_ref, v_ref, kk_ref, a_ref, lw_ref, g_ref, rk_ref, lnw_ref, lnb_ref,
                      o_ref, h_ref, *, passes):
    c = CHUNK
    c2 = 2 * c

    @pl.when(pl.program_id(1) == 0)
    def _():
        h_ref[...] = jnp.zeros_like(h_ref)

    lw = lw_ref[...]
    ti = lax.broadcasted_iota(jnp.int32, (c, c), 0)
    si = lax.broadcasted_iota(jnp.int32, (c, c), 1)
    tril = jnp.where(si <= ti, 1.0, 0.0)
    lwh, lwl = _split(lw)
    lwl2 = (lw - lwh.astype(F32) - lwl.astype(F32)).astype(BF16)
    trilb = tril.astype(BF16)
    cum = _dot(trilb, lwh) + _dot(trilb, lwl) + _dot(trilb, lwl2)
    p = jnp.exp(cum)
    pinv = jnp.exp(-cum)
    pprev = jnp.exp(cum - lw)
    pend = jnp.exp(cum[c - 1:c, :] - cum)
    r = r_ref[...]
    k = k_ref[...]
    v = v_ref[...]
    kk = kk_ref[...]
    b = kk * a_ref[...]
    at = _tiles(-kk * pprev)
    bt = _tiles(b * pinv)
    kt = _tiles(k * pinv)
    rt = _tiles(r * p)
    bh = _tiles(b * pend)
    kh = _tiles(k * pend)
    vt = _tiles(v)
    pc = _tiles(jnp.exp(cum[c - 1:c, :]))

    lo = lax.broadcasted_iota(jnp.int32, (1, 1, LANES), 2) < RWKV_HEAD

    def bd(x):
        return jnp.concatenate([jnp.where(lo, x, 0.0), jnp.where(lo, 0.0, x)], axis=1)

    a_bd, b_bd, k_bd, r_bd, v_bd = bd(at), bd(bt), bd(kt), bd(rt), bd(vt)
    bh_bd, kh_bd = bd(bh), bd(kh)

    amat = _bmm(jnp.concatenate([a_bd, r_bd], axis=1), jnp.concatenate([b_bd, k_bd], axis=1),
                "pil,pjl->pij", passes[0])
    ri = lax.broadcasted_iota(jnp.int32, (1, c2, c2), 1)
    ci = lax.broadcasted_iota(jnp.int32, (1, c2, c2), 2)
    same = (ri // c) == (ci // c)
    strict = same & ((ri % c) > (ci % c))
    incl = same & ((ri % c) >= (ci % c))
    a_ab = jnp.where(strict, amat[:, :c2, :c2], 0.0)
    a_ak = jnp.where(strict, amat[:, :c2, c2:], 0.0)
    a_rb = jnp.where(incl, amat[:, c2:, :c2], 0.0)
    a_rk = jnp.where(incl, amat[:, c2:, c2:], 0.0)

    eye = jnp.where(ri == ci, 1.0, 0.0)
    tinv = eye + a_ab
    apow = a_ab
    for _ in range(int(math.log2(c)) - 1):
        apow = _bmm(apow, apow, "pij,pjk->pik", passes[1])
        tinv = tinv + _bmm(tinv, apow, "pij,pjk->pik", passes[1])

    akv = _bmm(a_ak, v_bd, "pij,pjk->pik", passes[2])
    wu = _bmm(tinv, jnp.concatenate([a_bd, akv], axis=2), "pij,pjk->pik", passes[3])
    rhs = jnp.concatenate([wu, jnp.concatenate([jnp.zeros_like(v_bd), v_bd], axis=2)], axis=1)
    top = _bmm(jnp.concatenate([a_rb, a_rk], axis=2), rhs, "pij,pjk->pik", passes[4])
    bot = _bmm(jnp.concatenate([bh_bd, kh_bd], axis=1), rhs, "psk,psn->pkn", passes[4])
    q_eff = r_bd + top[:, :, :c2]
    o_intra = top[:, :, c2:]
    m_mat = eye * pc + bot[:, :, :c2]
    g_mat = bot[:, :, c2:]

    h0 = h_ref[...]
    seq_out = _bmm(jnp.concatenate([q_eff, m_mat], axis=1), h0, "pij,pjk->pik", passes[5])
    o_bd = seq_out[:, :c2, :] + o_intra
    h_ref[...] = seq_out[:, c2:, :] + g_mat
    o = jnp.where(lo, o_bd[:, :c, :], o_bd[:, c:, :])

    rk = rk_ref[...]
    lnw = lnw_ref[...]
    lnb = lnb_ref[...]
    g = g_ref[...]
    lo2 = lax.broadcasted_iota(jnp.int32, (c, LANES), 1) < RWKV_HEAD
    inv_n = 1.0 / RWKV_HEAD
    for t in range(RWKV_DIM // LANES):
        sl = slice(t * LANES, (t + 1) * LANES)
        ot = o[t]
        s0, s1 = _seg2(ot)
        cen = ot - jnp.where(lo2, s0, s1) * inv_n
        q0, q1 = _seg2(cen * cen)
        on = cen * lax.rsqrt(jnp.where(lo2, q0, q1) * inv_n + GN_EPS) * lnw[:, sl] + lnb[:, sl]
        b0, b1 = _seg2(r[:, sl] * k[:, sl] * rk[:, sl])
        on = on + jnp.where(lo2, b0, b1) * v[:, sl]
        o_ref[:, sl] = (on * g[:, sl]).astype(o_ref.dtype)


def _rwkv_scan(r, k, v, kk, a, lw, g, rk, lnw, lnb, batch, seq, passes):
    n, d = r.shape
    nck = seq // CHUNK
    blk = pl.BlockSpec((CHUNK, d), lambda bi, ci: (bi * nck + ci, 0))
    full = lambda x: pl.BlockSpec(x.shape, lambda bi, ci: (0,) * x.ndim)
    return pl.pallas_call(
        functools.partial(_rwkv_scan_kernel, passes=passes),
        grid=(batch, nck),
        in_specs=[blk] * 7 + [full(rk), full(lnw), full(lnb)],
        out_specs=blk,
        out_shape=jax.ShapeDtypeStruct((n, d), BF16),
        scratch_shapes=[pltpu.VMEM((d // LANES, LANES, LANES), F32)],
        compiler_params=_cparams(("parallel", "arbitrary")),
        name="rwkv_scan",
    )(r, k, v, kk, a, lw, g, rk, lnw, lnb)


def _bmm1(a, b, spec):
    return jnp.einsum(spec, a.astype(BF16), b.astype(BF16), preferred_element_type=F32)


def _rwkv_kernel(x_ref, gmix_ref, w_ref, mu_ref, w0_ref, ww_ref, a0_ref, wa_ref, wg_ref, kkp_ref, kap_ref,
                 rk_ref, lnw_ref, lnb_ref, o_ref, h_ref, zlast_ref):
    c = CHUNK
    c2 = 2 * c
    d = RWKV_DIM
    nt = d // LANES
    tb = x_ref.shape[0]
    nck = tb // c

    @pl.when(pl.program_id(1) == 0)
    def _():
        h_ref[...] = jnp.zeros_like(h_ref)
        zlast_ref[...] = jnp.zeros_like(zlast_ref)

    z = _dot(_rms(x_ref[...], gmix_ref[...]).astype(BF16), w_ref[...])
    rowid = lax.broadcasted_iota(jnp.int32, z.shape, 0)
    prev = jnp.where(rowid == 0, zlast_ref[...], pltpu.roll(z, 1, axis=0))
    zlast_ref[...] = z[tb - 1:tb, :]
    zs = z + (prev - z) * mu_ref[...]
    r = zs[:, :d]
    k0 = zs[:, d:2 * d]
    v = zs[:, 2 * d:3 * d]
    wa_lo = zs[:, 3 * d:3 * d + LANES]
    g_lo = zs[:, 3 * d + LANES:]
    y = -(w0_ref[...] + _dot(jnp.tanh(wa_lo).astype(BF16), ww_ref[...]))
    w = -(jnp.maximum(y, 0.0) + jnp.log(1.0 + jnp.exp(-jnp.abs(y)))) - 0.5
    lw = -jnp.exp(w)
    a = jax.nn.sigmoid(a0_ref[...] + _dot(wa_lo.astype(BF16), wa_ref[...]))
    g = _dot(jax.nn.sigmoid(g_lo).astype(BF16), wg_ref[...])
    k = k0 * (1.0 + (a - 1.0) * kap_ref[...])
    kk0 = k0 * kkp_ref[...]
    lo2 = lax.broadcasted_iota(jnp.int32, (tb, LANES), 1) < RWKV_HEAD
    kk_t = []
    for t in range(nt):
        kt = kk0[:, t * LANES:(t + 1) * LANES]
        s0, s1 = _seg2(kt * kt)
        kk_t.append(kt / jnp.where(lo2, jnp.maximum(jnp.sqrt(s0), 1e-12), jnp.maximum(jnp.sqrt(s1), 1e-12)))
    kk = jnp.concatenate(kk_t, axis=1)

    ti = lax.broadcasted_iota(jnp.int32, (tb, tb), 0)
    si = lax.broadcasted_iota(jnp.int32, (tb, tb), 1)
    tril = jnp.where((si <= ti) & (si // c == ti // c), 1.0, 0.0).astype(BF16)
    lw1 = lw.astype(BF16)
    lw2 = (lw - lw1.astype(F32)).astype(BF16)
    lw3 = (lw - lw1.astype(F32) - lw2.astype(F32)).astype(BF16)
    cum = _dot(tril, lw1) + _dot(tril, lw2) + _dot(tril, lw3)
    cend = jnp.concatenate([jnp.broadcast_to(cum[(ci + 1) * c - 1:(ci + 1) * c, :], (c, d)) for ci in range(nck)],
                           axis=0)
    p = jnp.exp(cum)
    pinv = jnp.exp(-cum)
    pprev = jnp.exp(cum - lw)
    pend = jnp.exp(cend - cum)
    b = kk * a

    qw = h_ref.shape[1]
    nq = d // qw
    hpq = qw // RWKV_HEAD
    nn = "pij,pjk->pik"

    def tiles(x):
        return jnp.stack([x[ci * c:(ci + 1) * c, q * qw:(q + 1) * qw]
                          for ci in range(nck) for q in range(nq)]).astype(BF16)

    lane_head = lax.broadcasted_iota(jnp.int32, (1, 1, qw), 2) // RWKV_HEAD

    def bdr(x):
        zero = jnp.zeros_like(x)
        return jnp.concatenate([jnp.where(lane_head == hd, x, zero) for hd in range(hpq)], axis=1)

    at, bt, kt, rt, vt = tiles(-kk * pprev), tiles(b * pinv), tiles(k * pinv), tiles(r * p), tiles(v)
    bht, kht = tiles(b * pend), tiles(k * pend)
    pc = jnp.stack([jnp.exp(cend[ci * c:ci * c + 1, q * qw:(q + 1) * qw])
                    for ci in range(nck) for q in range(nq)])

    amat = _bmm1(jnp.concatenate([at, rt], axis=1), jnp.concatenate([bdr(bt), bdr(kt)], axis=1), "pil,pjl->pij")
    trow = lax.broadcasted_iota(jnp.int32, (1, c, qw), 1)
    scol = lax.broadcasted_iota(jnp.int32, (1, c, qw), 2) % c
    strict = trow > scol
    incl = trow >= scol
    a_ab = jnp.where(strict, amat[:, :c, :qw], 0.0)
    a_ak = jnp.where(strict, amat[:, :c, qw:], 0.0)
    a_rb = jnp.where(incl, amat[:, c:, :qw], 0.0)
    a_rk = jnp.where(incl, amat[:, c:, qw:], 0.0)

    tinv = jnp.where(trow == scol, 1.0, 0.0) + a_ab
    apow = _bmm1(a_ab, bdr(a_ab.astype(BF16)), nn)
    levels = int(math.log2(c))
    for lvl in range(1, levels):
        pbd = bdr(apow.astype(BF16))
        if lvl < levels - 1:
            sq = _bmm1(jnp.concatenate([apow, tinv], axis=1), pbd, nn)
            apow, tinv = sq[:, :c], tinv + sq[:, c:]
        else:
            tinv = tinv + _bmm1(tinv, pbd, nn)

    akv = _bmm1(a_ak, bdr(vt), nn)
    wu = _bmm1(tinv, jnp.concatenate([bdr(at), bdr(akv.astype(BF16))], axis=2), nn)
    w_a = wu[:, :, :qw]
    u_v = wu[:, :, qw:]

    ht = h_ref[...]
    own = (lax.broadcasted_iota(jnp.int32, (1, qw, qw), 1) // RWKV_HEAD
           == lax.broadcasted_iota(jnp.int32, (1, qw, qw), 2) // RWKV_HEAD)
    o_ck = []
    for ci in range(nck):
        sl = slice(ci * nq, (ci + 1) * nq)
        s1 = _bmm1(jnp.concatenate([w_a[sl], rt[sl]], axis=1), ht, "pil,pjl->pij")
        u = (s1[:, :c] + u_v[sl]).astype(BF16)
        o_ck.append(s1[:, c:] + _bmm1(jnp.concatenate([a_rb[sl], a_rk[sl]], axis=2),
                                      jnp.concatenate([bdr(u), bdr(vt[sl])], axis=1), nn))
        upd = _bmm1(jnp.concatenate([u, vt[sl]], axis=1), jnp.concatenate([bht[sl], kht[sl]], axis=1),
                    "psv,psk->pvk")
        ht = ht * pc[sl] + jnp.where(own, upd, 0.0)
    h_ref[...] = ht

    rk = rk_ref[...]
    lnw = lnw_ref[...]
    lnb = lnb_ref[...]
    inv_n = 1.0 / RWKV_HEAD
    for t in range(nt):
        sl = slice(t * LANES, (t + 1) * LANES)
        qi, off = divmod(t * LANES, qw)
        ot = jnp.concatenate([o_ck[ci][qi][:, off:off + LANES] for ci in range(nck)], axis=0)
        s0, s1 = _seg2(ot)
        cen = ot - jnp.where(lo2, s0, s1) * inv_n
        q0, q1 = _seg2(cen * cen)
        on = cen * lax.rsqrt(jnp.where(lo2, q0, q1) * inv_n + GN_EPS) * lnw[:, sl] + lnb[:, sl]
        b0, b1 = _seg2(r[:, sl] * k[:, sl] * rk[:, sl])
        on = on + jnp.where(lo2, b0, b1) * v[:, sl]
        o_ref[:, sl] = (on * g[:, sl]).astype(o_ref.dtype)


def _rwkv(x, gmix, w, mu, w0, ww, a0, wa, wg, kkp, kap, rk, lnw, lnb, batch, seq, tb):
    n, d = x.shape
    spb = seq // tb
    res = lambda a: pl.BlockSpec(a.shape, lambda bi, si: (0,) * a.ndim, pipeline_mode=pl.Buffered(1))
    return pl.pallas_call(
        _rwkv_kernel,
        grid=(batch, spb),
        in_specs=[pl.BlockSpec((tb, d), lambda bi, si: (bi * spb + si, 0))]
        + [res(a) for a in (gmix, w, mu, w0, ww, a0, wa, wg, kkp, kap, rk, lnw, lnb)],
        out_specs=pl.BlockSpec((tb, RWKV_DIM), lambda bi, si: (bi * spb + si, 0)),
        out_shape=jax.ShapeDtypeStruct((n, RWKV_DIM), BF16),
        scratch_shapes=[pltpu.VMEM((RWKV_DIM // MXU_DIM, MXU_DIM, MXU_DIM), F32), pltpu.VMEM((1, w.shape[1]), F32)],
        compiler_params=_cparams(("parallel", "arbitrary")),
        name="rwkv",
    )(x, gmix, w, mu, w0, ww, a0, wa, wg, kkp, kap, rk, lnw, lnb)


def _resident(a):
    return pl.BlockSpec(a.shape, lambda i: (0,) * a.ndim, pipeline_mode=pl.Buffered(1))


def _out_proj_kernel(x_ref, oa_ref, ob_ref, w_ref, o_ref, w16_ref):
    @pl.when(pl.program_id(0) == 0)
    def _():
        w16_ref[...] = w_ref[...].astype(BF16)

    ka = oa_ref.shape[1]
    o_ref[...] = x_ref[...] + _dot(oa_ref[...], w16_ref[:ka, :]) + _dot(ob_ref[...], w16_ref[ka:, :])


def _out_proj(x, oa, ob, w, tm):
    n, d = x.shape
    row_blk = lambda a: pl.BlockSpec((tm, a.shape[1]), lambda i: (i, 0))
    return pl.pallas_call(
        _out_proj_kernel,
        grid=(n // tm,),
        in_specs=[row_blk(x), row_blk(oa), row_blk(ob), _resident(w)],
        out_specs=row_blk(x),
        out_shape=jax.ShapeDtypeStruct((n, d), F32),
        scratch_shapes=[pltpu.VMEM(w.shape, BF16)],
        compiler_params=_cparams(("arbitrary",)),
        name="out_proj",
    )(x, oa, ob, w)


def _ple_kernel(x_ref, g_ref, p_ref, wg_ref, wp_ref, o_ref, wg16_ref, wp16_ref):
    @pl.when(pl.program_id(0) == 0)
    def _():
        wg16_ref[...] = wg_ref[...].astype(BF16)
        wp16_ref[...] = wp_ref[...].astype(BF16)

    x = x_ref[...]
    gate = jax.nn.sigmoid(_dot(_rms(x, g_ref[...]).astype(BF16), wg16_ref[...]))
    o_ref[...] = x + gate * _dot(p_ref[...].astype(BF16), wp16_ref[...])


def _ple(x, g, p, wg, wp, tm):
    n, d = x.shape
    row_blk = lambda a: pl.BlockSpec((tm, a.shape[1]), lambda i: (i, 0))
    return pl.pallas_call(
        _ple_kernel,
        grid=(n // tm,),
        in_specs=[row_blk(x), _resident(g), row_blk(p), _resident(wg), _resident(wp)],
        out_specs=row_blk(x),
        out_shape=jax.ShapeDtypeStruct((n, d), F32),
        scratch_shapes=[pltpu.VMEM(wg.shape, BF16), pltpu.VMEM(wp.shape, BF16)],
        compiler_params=_cparams(("arbitrary",)),
        name="ple",
    )(x, g, p, wg, wp)


def _tile(n, pref):
    t = min(n, pref)
    assert n % t == 0, (n, t)
    return t


def _layer(x, p, norm_ffn1, w1_gate, w1_up, w1_down, norm_mix, w_in, q_a_norm, w_q_b, kv_a_norm, w_kv_b,
           q_norm, k_norm, mu_shift, w0, w_w2, a0, w_a2, w_g2, k_k, k_a, r_k, ln_x_w, ln_x_b, w_out,
           norm_ffn2, w2_gate, w2_up, w2_down, norm_ple, w_ple_gate, w_ple_proj):
    batch, seq, d = x.shape
    n = batch * seq
    row = lambda a: a.reshape(1, -1)
    bf = lambda a: a.astype(BF16)
    mla_in = Q_LORA + KV_LORA + QK_ROPE
    dff = w1_gate.shape[1]
    tf = _tile(dff, 256)
    tm_ffn = _tile(n, 1024)

    xf = x.reshape(n, d)
    x1 = _ffn(xf, row(norm_ffn1), w1_gate, w1_up, w1_down, tm_ffn, tf)

    w_mla = bf(jnp.concatenate([w_in[:, :mla_in], w_in[:, mla_in - QK_ROPE:mla_in]], axis=1))
    wq = w_q_b.reshape(Q_LORA, MLA_HEADS, QK_HEAD)
    wqb = bf(jnp.concatenate([wq[:, :, :QK_NOPE].reshape(Q_LORA, -1), wq[:, :, QK_NOPE:].reshape(Q_LORA, -1)], axis=1))
    inv_freq = 1.0 / (ROPE_BASE ** (jnp.arange(0, QK_ROPE, 2, dtype=F32) / QK_ROPE))
    ang = jnp.arange(seq, dtype=F32)[:, None] * inv_freq[None, :]
    cos, sin = jnp.cos(ang), jnp.sin(ang)
    cos_t = jnp.concatenate([cos, cos, cos, cos], axis=1)
    sin_t = jnp.concatenate([-sin, sin, -sin, sin], axis=1)
    pair = lambda a: row(jnp.concatenate([a, a]))
    tm_prep = _tile(seq, 512)
    q, k, v = _mla_prep(x1, row(norm_mix), w_mla, row(q_a_norm), wqb, row(kv_a_norm), bf(w_kv_b),
                        row(q_norm[:QK_NOPE]), pair(q_norm[QK_NOPE:]), row(k_norm[:QK_NOPE]),
                        pair(k_norm[QK_NOPE:]), cos_t, sin_t, batch, seq, tm_prep)
    o_mla = _flash(q, k, v, _tile(seq, 2048), _tile(seq, 512)).reshape(n, MLA_HEADS * V_HEAD)

    zero = jnp.zeros_like(w_w2)
    ww = bf(jnp.concatenate([w_w2, zero], axis=0))
    wa = bf(jnp.concatenate([zero, w_a2], axis=0))
    o_rwkv = _rwkv(x1, row(norm_mix), bf(w_in[:, mla_in:]), row(mu_shift), row(w0), ww, row(a0), wa, bf(w_g2),
                   row(k_k), row(k_a), row(r_k), row(ln_x_w), row(ln_x_b), batch, seq, _tile(seq, 4 * CHUNK))

    x2 = _out_proj(x1, o_mla, o_rwkv, w_out, _tile(n, 512))
    x3 = _ffn(x2, row(norm_ffn2), w2_gate, w2_up, w2_down, tm_ffn, tf)
    out = _ple(x3, row(norm_ple), p.reshape(n, -1), w_ple_gate, w_ple_proj, _tile(n, 256))
    return out.reshape(batch, seq, d)


def kernel(x, p, norm_ffn1, w1_gate, w1_up, w1_down, norm_mix, w_in, q_a_norm, w_q_b, kv_a_norm, w_kv_b, q_norm, k_norm, mu_shift, w0, w_w2, a0, w_a2, w_g2, k_k, k_a, r_k, ln_x_w, ln_x_b, w_out, norm_ffn2, w2_gate, w2_up, w2_down, norm_ple, w_ple_gate, w_ple_proj):
    depth = p.shape[0]
    for i in range(depth):
        x = _layer(x, p[i], norm_ffn1[i], w1_gate[i], w1_up[i], w1_down[i], norm_mix[i], w_in[i], q_a_norm[i],
                   w_q_b[i], kv_a_norm[i], w_kv_b[i], q_norm[i], k_norm[i], mu_shift[i], w0[i], w_w2[i], a0[i],
                   w_a2[i], w_g2[i], k_k[i], k_a[i], r_k[i], ln_x_w[i], ln_x_b[i], w_out[i], norm_ffn2[i],
                   w2_gate[i], w2_up[i], w2_down[i], norm_ple[i], w_ple_gate[i], w_ple_proj[i])
    return x
```

```python
import functools
import math

import jax
import jax.numpy as jnp
from jax import lax
from jax.experimental import pallas as pl
from jax.experimental.pallas import tpu as pltpu

F32 = jnp.float32
BF16 = jnp.bfloat16

EPS = 1e-6
GN_EPS = 64e-5
MLA_HEADS = 8
QK_NOPE = 128
QK_ROPE = 64
QK_HEAD = QK_NOPE + QK_ROPE
V_HEAD = 128
Q_LORA = 512
KV_LORA = 256
ROPE_BASE = 10000.0
RWKV_HEAD = 64
RWKV_HEADS = 16
RWKV_DIM = RWKV_HEADS * RWKV_HEAD
LANES = 128
MXU_DIM = 256
CHUNK = 64
VMEM_LIMIT = 56 * 1024 * 1024


def _cparams(sem):
    return pltpu.CompilerParams(dimension_semantics=sem, vmem_limit_bytes=VMEM_LIMIT)


def _dot(a, b):
    return jnp.dot(a, b, preferred_element_type=F32)


def _rms(x, g):
    return x * lax.rsqrt(jnp.mean(x * x, axis=-1, keepdims=True) + EPS) * g


def _resident(a):
    return pl.BlockSpec(a.shape, lambda *_: (0,) * a.ndim, pipeline_mode=pl.Buffered(1))


def _ffn_kernel(x_ref, g_ref, wg_ref, wu_ref, wd_ref, o_ref, h_ref):
    @pl.when(pl.program_id(1) == 0)
    def _():
        x = x_ref[...]
        h_ref[...] = _rms(x, g_ref[...]).astype(BF16)
        o_ref[...] = x

    h = h_ref[...]
    gate = _dot(h, wg_ref[...].astype(BF16))
    up = _dot(h, wu_ref[...].astype(BF16))
    act = (0.5 * gate * jax.nn.sigmoid(gate) * up).astype(BF16)
    o_ref[...] += _dot(act, wd_ref[...].astype(BF16))


def _ffn(x, g, wg, wu, wd, tm, tf):
    n, d = x.shape
    dff = wg.shape[1]
    return pl.pallas_call(
        _ffn_kernel,
        grid=(n // tm, dff // tf),
        in_specs=[
            pl.BlockSpec((tm, d), lambda i, j: (i, 0)),
            pl.BlockSpec((1, d), lambda i, j: (0, 0)),
            pl.BlockSpec((d, tf), lambda i, j: (0, j)),
            pl.BlockSpec((d, tf), lambda i, j: (0, j)),
            pl.BlockSpec((tf, d), lambda i, j: (j, 0)),
        ],
        out_specs=pl.BlockSpec((tm, d), lambda i, j: (i, 0)),
        out_shape=jax.ShapeDtypeStruct((n, d), F32),
        scratch_shapes=[pltpu.VMEM((tm, d), BF16)],
        compiler_params=_cparams(("parallel", "arbitrary")),
        name="ffn",
    )(x, g, wg, wu, wd)


def _seg2(x):
    lo = lax.broadcasted_iota(jnp.int32, x.shape, 1) < RWKV_HEAD
    s0 = jnp.sum(jnp.where(lo, x, 0.0), axis=-1, keepdims=True)
    s1 = jnp.sum(jnp.where(lo, 0.0, x), axis=-1, keepdims=True)
    return s0, s1


def _rope_pair(y, cos_t, sin_t):
    lane = lax.broadcasted_iota(jnp.int32, y.shape, 1)
    first = (lane % QK_ROPE) < (QK_ROPE // 2)
    rot = jnp.where(first, pltpu.roll(y, LANES - QK_ROPE // 2, axis=1), pltpu.roll(y, QK_ROPE // 2, axis=1))
    return y * cos_t + rot * sin_t


def _mla_prep_kernel(x_ref, gmix_ref, win_ref, qag_ref, wqb_ref, kvag_ref, wkvb_ref,
                     gqn_ref, gqp_ref, gkn_ref, gkp_ref, cos_ref, sin_ref,
                     q_ref, k_ref, v_ref):
    scale = 1.0 / math.sqrt(QK_HEAD)
    nope_w = MLA_HEADS * QK_NOPE
    tm = x_ref.shape[0]
    nsplit = 2 if tm % 16 == 0 else 1
    rows_per = tm // nsplit
    for part in range(nsplit):
        rows = slice(part * rows_per, (part + 1) * rows_per)
        h = _rms(x_ref[rows, :], gmix_ref[...]).astype(BF16)
        z = _dot(h, win_ref[...])
        q_lat = z[:, :Q_LORA]
        kv_lat = z[:, Q_LORA:Q_LORA + KV_LORA]
        kpe2 = z[:, Q_LORA + KV_LORA:]
        qf = _dot(_rms(q_lat, qag_ref[...]).astype(BF16), wqb_ref[...])
        kvf = _dot(_rms(kv_lat, kvag_ref[...]).astype(BF16), wkvb_ref[...])
        cos_t = cos_ref[rows, :]
        sin_t = sin_ref[rows, :]
        lo = lax.broadcasted_iota(jnp.int32, cos_t.shape, 1) < QK_ROPE
        kpe_ss, _ = _seg2(kpe2 * kpe2)
        kpe_rot = _rope_pair(kpe2 * gkp_ref[...], cos_t, sin_t)
        for j in range(MLA_HEADS // 2):
            qp = qf[:, nope_w + j * LANES: nope_w + (j + 1) * LANES]
            qs0, qs1 = _seg2(qp * qp)
            rs_q = []
            rs_k = []
            for e, qs in ((0, qs0), (1, qs1)):
                hd = 2 * j + e
                qn = qf[:, hd * QK_NOPE:(hd + 1) * QK_NOPE]
                rq = lax.rsqrt((jnp.sum(qn * qn, axis=-1, keepdims=True) + qs) * (1.0 / QK_HEAD) + EPS)
                q_ref[0, hd, rows, :QK_NOPE] = (qn * rq * gqn_ref[...] * scale).astype(BF16)
                rs_q.append(rq)
                kn = kvf[:, hd * 2 * QK_NOPE: hd * 2 * QK_NOPE + QK_NOPE]
                rk = lax.rsqrt((jnp.sum(kn * kn, axis=-1, keepdims=True) + kpe_ss) * (1.0 / QK_HEAD) + EPS)
                k_ref[0, hd, rows, :QK_NOPE] = (kn * rk * gkn_ref[...]).astype(BF16)
                rs_k.append(rk)
                v_ref[0, hd, rows, :] = kvf[:, hd * 2 * QK_NOPE + QK_NOPE:(hd + 1) * 2 * QK_NOPE].astype(BF16)
            yq = _rope_pair(qp * jnp.where(lo, rs_q[0], rs_q[1]) * gqp_ref[...], cos_t, sin_t) * scale
            q_ref[0, 2 * j, rows, QK_NOPE:] = jnp.where(lo, yq, 0.0).astype(BF16)
            q_ref[0, 2 * j + 1, rows, QK_NOPE:] = jnp.where(lo, 0.0, yq).astype(BF16)
            k_ref[0, 2 * j, rows, QK_NOPE:] = jnp.where(lo, kpe_rot * rs_k[0], 0.0).astype(BF16)
            k_ref[0, 2 * j + 1, rows, QK_NOPE:] = jnp.where(lo, 0.0, kpe_rot * rs_k[1]).astype(BF16)


def _mla_prep(x, gmix, win, qag, wqb, kvag, wkvb, gqn, gqp, gkn, gkp, cos_t, sin_t, batch, seq, tm):
    n, d = x.shape
    spt = seq // tm
    full = lambda a: pl.BlockSpec(a.shape, lambda i: (0,) * a.ndim)
    qk_w = 2 * LANES
    return pl.pallas_call(
        _mla_prep_kernel,
        grid=(n // tm,),
        in_specs=[
            pl.BlockSpec((tm, d), lambda i: (i, 0)),
            full(gmix), full(win), full(qag), full(wqb), full(kvag), full(wkvb),
            full(gqn), full(gqp), full(gkn), full(gkp),
            pl.BlockSpec((tm, LANES), lambda i: (i % spt, 0)),
            pl.BlockSpec((tm, LANES), lambda i: (i % spt, 0)),
        ],
        out_specs=[
            pl.BlockSpec((1, MLA_HEADS, tm, qk_w), lambda i: (i // spt, 0, i % spt, 0)),
            pl.BlockSpec((1, MLA_HEADS, tm, qk_w), lambda i: (i // spt, 0, i % spt, 0)),
            pl.BlockSpec((1, MLA_HEADS, tm, V_HEAD), lambda i: (i // spt, 0, i % spt, 0)),
        ],
        out_shape=[
            jax.ShapeDtypeStruct((batch, MLA_HEADS, seq, qk_w), BF16),
            jax.ShapeDtypeStruct((batch, MLA_HEADS, seq, qk_w), BF16),
            jax.ShapeDtypeStruct((batch, MLA_HEADS, seq, V_HEAD), BF16),
        ],
        compiler_params=_cparams(("parallel",)),
        name="mla_prep",
    )(x, gmix, win, qag, wqb, kvag, wkvb, gqn, gqp, gkn, gkp, cos_t, sin_t)


def _flash_kernel(q_ref, k_ref, v_ref, o_ref, m_ref, l_ref, acc_ref, *, tk, nsub):
    qi = pl.program_id(2)
    m_ref[...] = jnp.full_like(m_ref, -1e30)
    l_ref[...] = jnp.zeros_like(l_ref)
    acc_ref[...] = jnp.zeros_like(acc_ref)
    rep = tk // LANES
    causal = lax.broadcasted_iota(jnp.int32, (tk, tk), 1) <= lax.broadcasted_iota(jnp.int32, (tk, tk), 0)

    def step(sub, j, diag):
        rows = pl.ds(sub * tk, tk)
        kv = pl.ds(pl.multiple_of(j * tk, tk), tk)
        s = lax.dot_general(q_ref[0, 0, rows, :], k_ref[0, 0, kv, :], (((1,), (1,)), ((), ())),
                            preferred_element_type=F32)
        if diag:
            s = jnp.where(causal, s, -1e30)
        m_prev = m_ref[rows, :]
        m_new = jnp.maximum(m_prev, jnp.max(s, axis=-1, keepdims=True))
        alpha = jnp.exp(m_prev - m_new)
        p = jnp.exp(s - jnp.concatenate([m_new] * rep, axis=1))
        l_ref[rows, :] = alpha * l_ref[rows, :] + jnp.sum(p, axis=-1, keepdims=True)
        acc_ref[rows, :] = alpha * acc_ref[rows, :] + _dot(p.astype(BF16), v_ref[0, 0, kv, :])
        m_ref[rows, :] = m_new

    def body(j, carry):
        for sub in range(nsub):
            step(sub, j, False)
        return carry

    lax.fori_loop(0, qi * nsub, body, 0)
    for e in range(nsub):
        for sub in range(e, nsub):
            step(sub, qi * nsub + e, sub == e)
    o_ref[0] = (acc_ref[...] / l_ref[...]).astype(o_ref.dtype)


def _flash(q, k, v, tq, tk):
    b, hh, s, dk = q.shape
    dv = v.shape[-1]
    assert dv == LANES and tq % tk == 0
    return pl.pallas_call(
        functools.partial(_flash_kernel, tk=tk, nsub=tq // tk),
        grid=(b, hh, s // tq),
        in_specs=[
            pl.BlockSpec((1, 1, tq, dk), lambda bi, h, qi: (bi, h, qi, 0)),
            pl.BlockSpec((1, 1, s, dk), lambda bi, h, qi: (bi, h, 0, 0)),
            pl.BlockSpec((1, 1, s, dv), lambda bi, h, qi: (bi, h, 0, 0)),
        ],
        out_specs=pl.BlockSpec((1, tq, dv), lambda bi, h, qi: (bi, qi, h)),
        out_shape=jax.ShapeDtypeStruct((b, s, hh * dv), BF16),
        scratch_shapes=[pltpu.VMEM((tq, LANES), F32), pltpu.VMEM((tq, LANES), F32), pltpu.VMEM((tq, dv), F32)],
        compiler_params=_cparams(("parallel", "parallel", "arbitrary")),
        name="flash",
    )(q, k, v)


def _bmm1(a, b, spec):
    return jnp.einsum(spec, a.astype(BF16), b.astype(BF16), preferred_element_type=F32)


def _rwkv_kernel(x_ref, gmix_ref, w_ref, mu_ref, w0_ref, ww_ref, a0_ref, wa_ref, wg_ref, kkp_ref, kap_ref,
                 rk_ref, lnw_ref, lnb_ref, o_ref, h_ref, zlast_ref):
    c = CHUNK
    d = RWKV_DIM
    nt = d // LANES
    tb = x_ref.shape[0]
    nck = tb // c

    @pl.when(pl.program_id(1) == 0)
    def _():
        h_ref[...] = jnp.zeros_like(h_ref)
        zlast_ref[...] = jnp.zeros_like(zlast_ref)

    z = _dot(_rms(x_ref[...], gmix_ref[...]).astype(BF16), w_ref[...])
    rowid = lax.broadcasted_iota(jnp.int32, z.shape, 0)
    prev = jnp.where(rowid == 0, zlast_ref[...], pltpu.roll(z, 1, axis=0))
    zlast_ref[...] = z[tb - 1:tb, :]
    zs = z + (prev - z) * mu_ref[...]
    r = zs[:, :d]
    k0 = zs[:, d:2 * d]
    v = zs[:, 2 * d:3 * d]
    wa_lo = zs[:, 3 * d:3 * d + LANES]
    g_lo = zs[:, 3 * d + LANES:]
    y = -(w0_ref[...] + _dot(jnp.tanh(wa_lo).astype(BF16), ww_ref[...]))
    w = -(jnp.maximum(y, 0.0) + jnp.log(1.0 + jnp.exp(-jnp.abs(y)))) - 0.5
    lw = -jnp.exp(w)
    a = jax.nn.sigmoid(a0_ref[...] + _dot(wa_lo.astype(BF16), wa_ref[...]))
    g = _dot(jax.nn.sigmoid(g_lo).astype(BF16), wg_ref[...])
    k = k0 * (1.0 + (a - 1.0) * kap_ref[...])
    kk0 = k0 * kkp_ref[...]
    lo2 = lax.broadcasted_iota(jnp.int32, (tb, LANES), 1) < RWKV_HEAD
    kk_t = []
    for t in range(nt):
        kt = kk0[:, t * LANES:(t + 1) * LANES]
        s0, s1 = _seg2(kt * kt)
        kk_t.append(kt / jnp.where(lo2, jnp.maximum(jnp.sqrt(s0), 1e-12), jnp.maximum(jnp.sqrt(s1), 1e-12)))
    kk = jnp.concatenate(kk_t, axis=1)

    ti = lax.broadcasted_iota(jnp.int32, (tb, tb), 0)
    si = lax.broadcasted_iota(jnp.int32, (tb, tb), 1)
    tril = jnp.where((si <= ti) & (si // c == ti // c), 1.0, 0.0).astype(BF16)
    lw1 = lw.astype(BF16)
    lw2 = (lw - lw1.astype(F32)).astype(BF16)
    lw3 = (lw - lw1.astype(F32) - lw2.astype(F32)).astype(BF16)
    cum = _dot(tril, lw1) + _dot(tril, lw2) + _dot(tril, lw3)
    cend = jnp.concatenate([jnp.broadcast_to(cum[(ci + 1) * c - 1:(ci + 1) * c, :], (c, d)) for ci in range(nck)],
                           axis=0)
    p = jnp.exp(cum)
    pinv = jnp.exp(-cum)
    pprev = jnp.exp(cum - lw)
    pend = jnp.exp(cend - cum)
    b = kk * a

    qw = h_ref.shape[1]
    nq = d // qw
    hpq = qw // RWKV_HEAD
    nn = "pij,pjk->pik"

    def tiles(x):
        return jnp.stack([x[ci * c:(ci + 1) * c, q * qw:(q + 1) * qw]
                          for ci in range(nck) for q in range(nq)]).astype(BF16)

    lane_head = lax.broadcasted_iota(jnp.int32, (1, 1, qw), 2) // RWKV_HEAD

    def bdr(x):
        zero = jnp.zeros_like(x)
        return jnp.concatenate([jnp.where(lane_head == hd, x, zero) for hd in range(hpq)], axis=1)

    at, bt, kt, rt, vt = tiles(-kk * pprev), tiles(b * pinv), tiles(k * pinv), tiles(r * p), tiles(v)
    bht, kht = tiles(b * pend), tiles(k * pend)
    pc = jnp.stack([jnp.exp(cend[ci * c:ci * c + 1, q * qw:(q + 1) * qw])
                    for ci in range(nck) for q in range(nq)])

    amat = _bmm1(jnp.concatenate([at, rt], axis=1), jnp.concatenate([bdr(bt), bdr(kt)], axis=1), "pil,pjl->pij")
    trow = lax.broadcasted_iota(jnp.int32, (1, c, qw), 1)
    scol = lax.broadcasted_iota(jnp.int32, (1, c, qw), 2) % c
    strict = trow > scol
    incl = trow >= scol
    a_ab = jnp.where(strict, amat[:, :c, :qw], 0.0)
    a_ak = jnp.where(strict, amat[:, :c, qw:], 0.0)
    a_rb = jnp.where(incl, amat[:, c:, :qw], 0.0)
    a_rk = jnp.where(incl, amat[:, c:, qw:], 0.0)

    tinv = jnp.where(trow == scol, 1.0, 0.0) + a_ab
    apow = _bmm1(a_ab, bdr(a_ab.astype(BF16)), nn)
    levels = int(math.log2(c))
    for lvl in range(1, levels):
        pbd = bdr(apow.astype(BF16))
        if lvl < levels - 1:
            sq = _bmm1(jnp.concatenate([apow, tinv], axis=1), pbd, nn)
            apow, tinv = sq[:, :c], tinv + sq[:, c:]
        else:
            tinv = tinv + _bmm1(tinv, pbd, nn)

    akv = _bmm1(a_ak, bdr(vt), nn)
    wu = _bmm1(tinv, jnp.concatenate([bdr(at), bdr(akv.astype(BF16))], axis=2), nn)
    w_a = wu[:, :, :qw]
    u_v = wu[:, :, qw:]

    ht = h_ref[...]
    own = (lax.broadcasted_iota(jnp.int32, (1, qw, qw), 1) // RWKV_HEAD
           == lax.broadcasted_iota(jnp.int32, (1, qw, qw), 2) // RWKV_HEAD)
    o_ck = []
    for ci in range(nck):
        sl = slice(ci * nq, (ci + 1) * nq)
        s1 = _bmm1(jnp.concatenate([w_a[sl], rt[sl]], axis=1), ht, "pil,pjl->pij")
        u = (s1[:, :c] + u_v[sl]).astype(BF16)
        o_ck.append(s1[:, c:] + _bmm1(jnp.concatenate([a_rb[sl], a_rk[sl]], axis=2),
                                      jnp.concatenate([bdr(u), bdr(vt[sl])], axis=1), nn))
        upd = _bmm1(jnp.concatenate([u, vt[sl]], axis=1), jnp.concatenate([bht[sl], kht[sl]], axis=1),
                    "psv,psk->pvk")
        ht = ht * pc[sl] + jnp.where(own, upd, 0.0)
    h_ref[...] = ht

    rk = rk_ref[...]
    lnw = lnw_ref[...]
    lnb = lnb_ref[...]
    inv_n = 1.0 / RWKV_HEAD
    for t in range(nt):
        sl = slice(t * LANES, (t + 1) * LANES)
        qi, off = divmod(t * LANES, qw)
        ot = jnp.concatenate([o_ck[ci][qi][:, off:off + LANES] for ci in range(nck)], axis=0)
        s0, s1 = _seg2(ot)
        cen = ot - jnp.where(lo2, s0, s1) * inv_n
        q0, q1 = _seg2(cen * cen)
        on = cen * lax.rsqrt(jnp.where(lo2, q0, q1) * inv_n + GN_EPS) * lnw[:, sl] + lnb[:, sl]
        b0, b1 = _seg2(r[:, sl] * k[:, sl] * rk[:, sl])
        on = on + jnp.where(lo2, b0, b1) * v[:, sl]
        o_ref[:, sl] = (on * g[:, sl]).astype(o_ref.dtype)


def _rwkv(x, gmix, w, mu, w0, ww, a0, wa, wg, kkp, kap, rk, lnw, lnb, batch, seq, tb):
    n, d = x.shape
    spb = seq // tb
    return pl.pallas_call(
        _rwkv_kernel,
        grid=(batch, spb),
        in_specs=[pl.BlockSpec((tb, d), lambda bi, si: (bi * spb + si, 0))]
        + [_resident(a) for a in (gmix, w, mu, w0, ww, a0, wa, wg, kkp, kap, rk, lnw, lnb)],
        out_specs=pl.BlockSpec((tb, RWKV_DIM), lambda bi, si: (bi * spb + si, 0)),
        out_shape=jax.ShapeDtypeStruct((n, RWKV_DIM), BF16),
        scratch_shapes=[pltpu.VMEM((RWKV_DIM // MXU_DIM, MXU_DIM, MXU_DIM), F32), pltpu.VMEM((1, w.shape[1]), F32)],
        compiler_params=_cparams(("parallel", "arbitrary")),
        name="rwkv",
    )(x, gmix, w, mu, w0, ww, a0, wa, wg, kkp, kap, rk, lnw, lnb)


def _out_proj_kernel(x_ref, oa_ref, ob_ref, w_ref, o_ref, w16_ref):
    @pl.when(pl.program_id(0) == 0)
    def _():
        w16_ref[...] = w_ref[...].astype(BF16)

    ka = oa_ref.shape[1]
    o_ref[...] = x_ref[...] + _dot(oa_ref[...], w16_ref[:ka, :]) + _dot(ob_ref[...], w16_ref[ka:, :])


def _out_proj(x, oa, ob, w, tm):
    n, d = x.shape
    row_blk = lambda a: pl.BlockSpec((tm, a.shape[1]), lambda i: (i, 0))
    return pl.pallas_call(
        _out_proj_kernel,
        grid=(n // tm,),
        in_specs=[row_blk(x), row_blk(oa), row_blk(ob), _resident(w)],
        out_specs=row_blk(x),
        out_shape=jax.ShapeDtypeStruct((n, d), F32),
        scratch_shapes=[pltpu.VMEM(w.shape, BF16)],
        compiler_params=_cparams(("arbitrary",)),
        name="out_proj",
    )(x, oa, ob, w)


def _ple_kernel(x_ref, g_ref, p_ref, wg_ref, wp_ref, o_ref, wg16_ref, wp16_ref):
    @pl.when(pl.program_id(0) == 0)
    def _():
        wg16_ref[...] = wg_ref[...].astype(BF16)
        wp16_ref[...] = wp_ref[...].astype(BF16)

    x = x_ref[...]
    gate = jax.nn.sigmoid(_dot(_rms(x, g_ref[...]).astype(BF16), wg16_ref[...]))
    o_ref[...] = x + gate * _dot(p_ref[...].astype(BF16), wp16_ref[...])


def _ple(x, g, p, wg, wp, tm):
    n, d = x.shape
    row_blk = lambda a: pl.BlockSpec((tm, a.shape[1]), lambda i: (i, 0))
    return pl.pallas_call(
        _ple_kernel,
        grid=(n // tm,),
        in_specs=[row_blk(x), _resident(g), row_blk(p), _resident(wg), _resident(wp)],
        out_specs=row_blk(x),
        out_shape=jax.ShapeDtypeStruct((n, d), F32),
        scratch_shapes=[pltpu.VMEM(wg.shape, BF16), pltpu.VMEM(wp.shape, BF16)],
        compiler_params=_cparams(("arbitrary",)),
        name="ple",
    )(x, g, p, wg, wp)


def _tile(n, pref):
    t = min(n, pref)
    assert n % t == 0, (n, t)
    return t


def _layer(x, p, norm_ffn1, w1_gate, w1_up, w1_down, norm_mix, w_in, q_a_norm, w_q_b, kv_a_norm, w_kv_b,
           q_norm, k_norm, mu_shift, w0, w_w2, a0, w_a2, w_g2, k_k, k_a, r_k, ln_x_w, ln_x_b, w_out,
           norm_ffn2, w2_gate, w2_up, w2_down, norm_ple, w_ple_gate, w_ple_proj):
    batch, seq, d = x.shape
    n = batch * seq
    row = lambda a: a.reshape(1, -1)
    bf = lambda a: a.astype(BF16)
    mla_in = Q_LORA + KV_LORA + QK_ROPE
    dff = w1_gate.shape[1]
    tf = _tile(dff, 256)
    tm_ffn = _tile(n, 1024)

    xf = x.reshape(n, d)
    x1 = _ffn(xf, row(norm_ffn1), w1_gate, w1_up, w1_down, tm_ffn, tf)

    w_mla = bf(jnp.concatenate([w_in[:, :mla_in], w_in[:, mla_in - QK_ROPE:mla_in]], axis=1))
    wq = w_q_b.reshape(Q_LORA, MLA_HEADS, QK_HEAD)
    wqb = bf(jnp.concatenate([wq[:, :, :QK_NOPE].reshape(Q_LORA, -1), wq[:, :, QK_NOPE:].reshape(Q_LORA, -1)], axis=1))
    inv_freq = 1.0 / (ROPE_BASE ** (jnp.arange(0, QK_ROPE, 2, dtype=F32) / QK_ROPE))
    ang = jnp.arange(seq, dtype=F32)[:, None] * inv_freq[None, :]
    cos, sin = jnp.cos(ang), jnp.sin(ang)
    cos_t = jnp.concatenate([cos, cos, cos, cos], axis=1)
    sin_t = jnp.concatenate([-sin, sin, -sin, sin], axis=1)
    pair = lambda a: row(jnp.concatenate([a, a]))
    q, k, v = _mla_prep(x1, row(norm_mix), w_mla, row(q_a_norm), wqb, row(kv_a_norm), bf(w_kv_b),
                        row(q_norm[:QK_NOPE]), pair(q_norm[QK_NOPE:]), row(k_norm[:QK_NOPE]),
                        pair(k_norm[QK_NOPE:]), cos_t, sin_t, batch, seq, _tile(seq, 512))
    o_mla = _flash(q, k, v, _tile(seq, 2048), _tile(seq, 512)).reshape(n, MLA_HEADS * V_HEAD)

    zero = jnp.zeros_like(w_w2)
    ww = bf(jnp.concatenate([w_w2, zero], axis=0))
    wa = bf(jnp.concatenate([zero, w_a2], axis=0))
    o_rwkv = _rwkv(x1, row(norm_mix), bf(w_in[:, mla_in:]), row(mu_shift), row(w0), ww, row(a0), wa, bf(w_g2),
                   row(k_k), row(k_a), row(r_k), row(ln_x_w), row(ln_x_b), batch, seq, _tile(seq, 4 * CHUNK))

    x2 = _out_proj(x1, o_mla, o_rwkv, w_out, _tile(n, 512))
    x3 = _ffn(x2, row(norm_ffn2), w2_gate, w2_up, w2_down, tm_ffn, tf)
    out = _ple(x3, row(norm_ple), p.reshape(n, -1), w_ple_gate, w_ple_proj, _tile(n, 256))
    return out.reshape(batch, seq, d)


def kernel(x, p, norm_ffn1, w1_gate, w1_up, w1_down, norm_mix, w_in, q_a_norm, w_q_b, kv_a_norm, w_kv_b, q_norm, k_norm, mu_shift, w0, w_w2, a0, w_a2, w_g2, k_k, k_a, r_k, ln_x_w, ln_x_b, w_out, norm_ffn2, w2_gate, w2_up, w2_down, norm_ple, w_ple_gate, w_ple_proj):
    depth = p.shape[0]
    for i in range(depth):
        x = _layer(x, p[i], norm_ffn1[i], w1_gate[i], w1_up[i], w1_down[i], norm_mix[i], w_in[i], q_a_norm[i],
                   w_q_b[i], kv_a_norm[i], w_kv_b[i], q_norm[i], k_norm[i], mu_shift[i], w0[i], w_w2[i], a0[i],
                   w_a2[i], w_g2[i], k_k[i], k_a[i], r_k[i], ln_x_w[i], ln_x_b[i], w_out[i], norm_ffn2[i],
                   w2_gate[i], w2_up[i], w2_down[i], norm_ple[i], w_ple_gate[i], w_ple_proj[i])
    return x
```

```python
import functools
import math

import jax
import jax.numpy as jnp
from jax import lax
from jax.experimental import pallas as pl
from jax.experimental.pallas import tpu as pltpu

F32 = jnp.float32
BF16 = jnp.bfloat16

EPS = 1e-6
GN_EPS = 64e-5
MLA_HEADS = 8
QK_NOPE = 128
QK_ROPE = 64
QK_HEAD = QK_NOPE + QK_ROPE
V_HEAD = 128
Q_LORA = 512
KV_LORA = 256
ROPE_BASE = 10000.0
RWKV_HEAD = 64
RWKV_HEADS = 16
RWKV_DIM = RWKV_HEADS * RWKV_HEAD
LANES = 128
MXU_DIM = 256
CHUNK = 64
VMEM_LIMIT = 56 * 1024 * 1024


def _cparams(sem):
    return pltpu.CompilerParams(dimension_semantics=sem, vmem_limit_bytes=VMEM_LIMIT)


def _dot(a, b):
    return jnp.dot(a, b, preferred_element_type=F32)


def _rms(x, g):
    return x * lax.rsqrt(jnp.mean(x * x, axis=-1, keepdims=True) + EPS) * g


def _resident(a):
    return pl.BlockSpec(a.shape, lambda *_: (0,) * a.ndim, pipeline_mode=pl.Buffered(1))


def _ffn_kernel(x_ref, g_ref, wg_ref, wu_ref, wd_ref, o_ref, h_ref):
    @pl.when(pl.program_id(1) == 0)
    def _():
        x = x_ref[...]
        h_ref[...] = _rms(x, g_ref[...]).astype(BF16)
        o_ref[...] = x

    h = h_ref[...]
    gate = _dot(h, wg_ref[...].astype(BF16))
    up = _dot(h, wu_ref[...].astype(BF16))
    act = (0.5 * gate * jax.nn.sigmoid(gate) * up).astype(BF16)
    o_ref[...] += _dot(act, wd_ref[...].astype(BF16))


def _ffn(x, g, wg, wu, wd, tm, tf):
    n, d = x.shape
    dff = wg.shape[1]
    return pl.pallas_call(
        _ffn_kernel,
        grid=(n // tm, dff // tf),
        in_specs=[
            pl.BlockSpec((tm, d), lambda i, j: (i, 0)),
            pl.BlockSpec((1, d), lambda i, j: (0, 0)),
            pl.BlockSpec((d, tf), lambda i, j: (0, j)),
            pl.BlockSpec((d, tf), lambda i, j: (0, j)),
            pl.BlockSpec((tf, d), lambda i, j: (j, 0)),
        ],
        out_specs=pl.BlockSpec((tm, d), lambda i, j: (i, 0)),
        out_shape=jax.ShapeDtypeStruct((n, d), F32),
        scratch_shapes=[pltpu.VMEM((tm, d), BF16)],
        compiler_params=_cparams(("parallel", "arbitrary")),
        name="ffn",
    )(x, g, wg, wu, wd)


def _seg2(x):
    lo = lax.broadcasted_iota(jnp.int32, x.shape, 1) < RWKV_HEAD
    s0 = jnp.sum(jnp.where(lo, x, 0.0), axis=-1, keepdims=True)
    s1 = jnp.sum(jnp.where(lo, 0.0, x), axis=-1, keepdims=True)
    return s0, s1


def _rope_pair(y, cos_t, sin_t):
    lane = lax.broadcasted_iota(jnp.int32, y.shape, 1)
    first = (lane % QK_ROPE) < (QK_ROPE // 2)
    rot = jnp.where(first, pltpu.roll(y, LANES - QK_ROPE // 2, axis=1), pltpu.roll(y, QK_ROPE // 2, axis=1))
    return y * cos_t + rot * sin_t


def _mla_prep_kernel(x_ref, gmix_ref, win_ref, qag_ref, wqb_ref, kvag_ref, wkvb_ref,
                     gqn_ref, gqp_ref, gkn_ref, gkp_ref, cos_ref, sin_ref,
                     q_ref, k_ref, v_ref):
    scale = math.log2(math.e) / math.sqrt(QK_HEAD)
    nope_w = MLA_HEADS * QK_NOPE
    tm = x_ref.shape[0]
    nsplit = 2 if tm % 16 == 0 else 1
    rows_per = tm // nsplit
    for part in range(nsplit):
        rows = slice(part * rows_per, (part + 1) * rows_per)
        h = _rms(x_ref[rows, :], gmix_ref[...]).astype(BF16)
        z = _dot(h, win_ref[...])
        q_lat = z[:, :Q_LORA]
        kv_lat = z[:, Q_LORA:Q_LORA + KV_LORA]
        kpe2 = z[:, Q_LORA + KV_LORA:]
        qf = _dot(_rms(q_lat, qag_ref[...]).astype(BF16), wqb_ref[...])
        kvf = _dot(_rms(kv_lat, kvag_ref[...]).astype(BF16), wkvb_ref[...])
        cos_t = cos_ref[rows, :]
        sin_t = sin_ref[rows, :]
        lo = lax.broadcasted_iota(jnp.int32, cos_t.shape, 1) < QK_ROPE
        kpe_ss, _ = _seg2(kpe2 * kpe2)
        kpe_rot = _rope_pair(kpe2 * gkp_ref[...], cos_t, sin_t)
        for j in range(MLA_HEADS // 2):
            qp = qf[:, nope_w + j * LANES: nope_w + (j + 1) * LANES]
            qs0, qs1 = _seg2(qp * qp)
            rs_q = []
            rs_k = []
            for e, qs in ((0, qs0), (1, qs1)):
                hd = 2 * j + e
                qn = qf[:, hd * QK_NOPE:(hd + 1) * QK_NOPE]
                rq = lax.rsqrt((jnp.sum(qn * qn, axis=-1, keepdims=True) + qs) * (1.0 / QK_HEAD) + EPS)
                q_ref[0, hd, rows, :QK_NOPE] = (qn * rq * gqn_ref[...] * scale).astype(BF16)
                rs_q.append(rq)
                kn = kvf[:, hd * 2 * QK_NOPE: hd * 2 * QK_NOPE + QK_NOPE]
                rk = lax.rsqrt((jnp.sum(kn * kn, axis=-1, keepdims=True) + kpe_ss) * (1.0 / QK_HEAD) + EPS)
                k_ref[0, hd, rows, :QK_NOPE] = (kn * rk * gkn_ref[...]).astype(BF16)
                rs_k.append(rk)
                v_ref[0, hd, rows, :] = kvf[:, hd * 2 * QK_NOPE + QK_NOPE:(hd + 1) * 2 * QK_NOPE].astype(BF16)
            yq = _rope_pair(qp * jnp.where(lo, rs_q[0], rs_q[1]) * gqp_ref[...], cos_t, sin_t) * scale
            q_ref[0, 2 * j, rows, QK_NOPE:] = jnp.where(lo, yq, 0.0).astype(BF16)
            q_ref[0, 2 * j + 1, rows, QK_NOPE:] = jnp.where(lo, 0.0, yq).astype(BF16)
            k_ref[0, 2 * j, rows, QK_NOPE:] = jnp.where(lo, kpe_rot * rs_k[0], 0.0).astype(BF16)
            k_ref[0, 2 * j + 1, rows, QK_NOPE:] = jnp.where(lo, 0.0, kpe_rot * rs_k[1]).astype(BF16)


def _mla_prep(x, gmix, win, qag, wqb, kvag, wkvb, gqn, gqp, gkn, gkp, cos_t, sin_t, batch, seq, tm):
    n, d = x.shape
    spt = seq // tm
    full = lambda a: pl.BlockSpec(a.shape, lambda i: (0,) * a.ndim)
    qk_w = 2 * LANES
    return pl.pallas_call(
        _mla_prep_kernel,
        grid=(n // tm,),
        in_specs=[
            pl.BlockSpec((tm, d), lambda i: (i, 0)),
            full(gmix), full(win), full(qag), full(wqb), full(kvag), full(wkvb),
            full(gqn), full(gqp), full(gkn), full(gkp),
            pl.BlockSpec((tm, LANES), lambda i: (i % spt, 0)),
            pl.BlockSpec((tm, LANES), lambda i: (i % spt, 0)),
        ],
        out_specs=[
            pl.BlockSpec((1, MLA_HEADS, tm, qk_w), lambda i: (i // spt, 0, i % spt, 0)),
            pl.BlockSpec((1, MLA_HEADS, tm, qk_w), lambda i: (i // spt, 0, i % spt, 0)),
            pl.BlockSpec((1, MLA_HEADS, tm, V_HEAD), lambda i: (i // spt, 0, i % spt, 0)),
        ],
        out_shape=[
            jax.ShapeDtypeStruct((batch, MLA_HEADS, seq, qk_w), BF16),
            jax.ShapeDtypeStruct((batch, MLA_HEADS, seq, qk_w), BF16),
            jax.ShapeDtypeStruct((batch, MLA_HEADS, seq, V_HEAD), BF16),
        ],
        compiler_params=_cparams(("parallel",)),
        name="mla_prep",
    )(x, gmix, win, qag, wqb, kvag, wkvb, gqn, gqp, gkn, gkp, cos_t, sin_t)


def _flash_kernel(q_ref, k_ref, v_ref, o_ref, m_ref, l_ref, acc_ref, *, tk, nsub):
    qi = pl.program_id(2)
    m_ref[...] = jnp.full_like(m_ref, -1e30)
    l_ref[...] = jnp.zeros_like(l_ref)
    acc_ref[...] = jnp.zeros_like(acc_ref)
    rep = tk // LANES
    causal = lax.broadcasted_iota(jnp.int32, (tk, tk), 1) <= lax.broadcasted_iota(jnp.int32, (tk, tk), 0)

    def step(sub, j, diag):
        rows = pl.ds(sub * tk, tk)
        kv = pl.ds(pl.multiple_of(j * tk, tk), tk)
        s = lax.dot_general(q_ref[0, 0, rows, :], k_ref[0, 0, kv, :], (((1,), (1,)), ((), ())),
                            preferred_element_type=F32)
        if diag:
            s = jnp.where(causal, s, -1e30)
        m_prev = m_ref[rows, :]
        m_new = jnp.maximum(m_prev, jnp.max(s, axis=-1, keepdims=True))
        alpha = jnp.exp2(m_prev - m_new)
        p = jnp.exp2(s - jnp.concatenate([m_new] * rep, axis=1))
        l_ref[rows, :] = alpha * l_ref[rows, :] + jnp.sum(p, axis=-1, keepdims=True)
        acc_ref[rows, :] = alpha * acc_ref[rows, :] + _dot(p.astype(BF16), v_ref[0, 0, kv, :])
        m_ref[rows, :] = m_new

    def body(j, carry):
        for sub in range(nsub):
            step(sub, j, False)
        return carry

    lax.fori_loop(0, qi * nsub, body, 0)
    for e in range(nsub):
        for sub in range(e, nsub):
            step(sub, qi * nsub + e, sub == e)
    o_ref[0] = (acc_ref[...] / l_ref[...]).astype(o_ref.dtype)


def _flash(q, k, v, tq, tk):
    b, hh, s, dk = q.shape
    dv = v.shape[-1]
    assert dv == LANES and tq % tk == 0
    return pl.pallas_call(
        functools.partial(_flash_kernel, tk=tk, nsub=tq // tk),
        grid=(b, hh, s // tq),
        in_specs=[
            pl.BlockSpec((1, 1, tq, dk), lambda bi, h, qi: (bi, h, qi, 0)),
            pl.BlockSpec((1, 1, s, dk), lambda bi, h, qi: (bi, h, 0, 0)),
            pl.BlockSpec((1, 1, s, dv), lambda bi, h, qi: (bi, h, 0, 0)),
        ],
        out_specs=pl.BlockSpec((1, tq, dv), lambda bi, h, qi: (bi, qi, h)),
        out_shape=jax.ShapeDtypeStruct((b, s, hh * dv), BF16),
        scratch_shapes=[pltpu.VMEM((tq, LANES), F32), pltpu.VMEM((tq, LANES), F32), pltpu.VMEM((tq, dv), F32)],
        compiler_params=_cparams(("parallel", "parallel", "arbitrary")),
        name="flash",
    )(q, k, v)


def _bmm1(a, b, spec):
    return jnp.einsum(spec, a.astype(BF16), b.astype(BF16), preferred_element_type=F32)


def _rwkv_kernel(x_ref, gmix_ref, w_ref, mu_ref, w0_ref, ww_ref, a0_ref, wa_ref, wg_ref, kkp_ref, kap_ref,
                 rk_ref, lnw_ref, lnb_ref, o_ref, h_ref, zlast_ref):
    c = CHUNK
    d = RWKV_DIM
    nt = d // LANES
    tb = x_ref.shape[0]
    nck = tb // c

    @pl.when(pl.program_id(1) == 0)
    def _():
        h_ref[...] = jnp.zeros_like(h_ref)
        zlast_ref[...] = jnp.zeros_like(zlast_ref)

    z = _dot(_rms(x_ref[...], gmix_ref[...]).astype(BF16), w_ref[...])
    rowid = lax.broadcasted_iota(jnp.int32, z.shape, 0)
    prev = jnp.where(rowid == 0, zlast_ref[...], pltpu.roll(z, 1, axis=0))
    zlast_ref[...] = z[tb - 1:tb, :]
    zs = z + (prev - z) * mu_ref[...]
    r = zs[:, :d]
    k0 = zs[:, d:2 * d]
    v = zs[:, 2 * d:3 * d]
    wa_lo = zs[:, 3 * d:3 * d + LANES]
    g_lo = zs[:, 3 * d + LANES:]
    y = -(w0_ref[...] + _dot(jnp.tanh(wa_lo).astype(BF16), ww_ref[...]))
    w = -(jnp.maximum(y, 0.0) + jnp.log(1.0 + jnp.exp(-jnp.abs(y)))) - 0.5
    lw = -jnp.exp(w)
    a = jax.nn.sigmoid(a0_ref[...] + _dot(wa_lo.astype(BF16), wa_ref[...]))
    g = _dot(jax.nn.sigmoid(g_lo).astype(BF16), wg_ref[...])
    k = k0 * (1.0 + (a - 1.0) * kap_ref[...])
    kk0 = k0 * kkp_ref[...]
    lo2 = lax.broadcasted_iota(jnp.int32, (tb, LANES), 1) < RWKV_HEAD
    kk_t = []
    for t in range(nt):
        kt = kk0[:, t * LANES:(t + 1) * LANES]
        s0, s1 = _seg2(kt * kt)
        kk_t.append(kt / jnp.where(lo2, jnp.maximum(jnp.sqrt(s0), 1e-12), jnp.maximum(jnp.sqrt(s1), 1e-12)))
    kk = jnp.concatenate(kk_t, axis=1)

    ti = lax.broadcasted_iota(jnp.int32, (tb, tb), 0)
    si = lax.broadcasted_iota(jnp.int32, (tb, tb), 1)
    tril = jnp.where((si <= ti) & (si // c == ti // c), 1.0, 0.0).astype(BF16)
    lw1 = lw.astype(BF16)
    lw2 = (lw - lw1.astype(F32)).astype(BF16)
    lw3 = (lw - lw1.astype(F32) - lw2.astype(F32)).astype(BF16)
    cum = _dot(tril, lw1) + _dot(tril, lw2) + _dot(tril, lw3)
    cend = jnp.concatenate([jnp.broadcast_to(cum[(ci + 1) * c - 1:(ci + 1) * c, :], (c, d)) for ci in range(nck)],
                           axis=0)
    p = jnp.exp(cum)
    pinv = jnp.exp(-cum)
    pprev = jnp.exp(cum - lw)
    pend = jnp.exp(cend - cum)
    b = kk * a

    qw = h_ref.shape[1]
    nq = d // qw
    hpq = qw // RWKV_HEAD
    nn = "pij,pjk->pik"

    def tiles(x):
        return jnp.stack([x[ci * c:(ci + 1) * c, q * qw:(q + 1) * qw]
                          for ci in range(nck) for q in range(nq)]).astype(BF16)

    lane_head = lax.broadcasted_iota(jnp.int32, (1, 1, qw), 2) // RWKV_HEAD

    def bdr(x):
        zero = jnp.zeros_like(x)
        return jnp.concatenate([jnp.where(lane_head == hd, x, zero) for hd in range(hpq)], axis=1)

    at, bt, kt, rt, vt = tiles(-kk * pprev), tiles(b * pinv), tiles(k * pinv), tiles(r * p), tiles(v)
    bht, kht = tiles(b * pend), tiles(k * pend)
    pc = jnp.stack([jnp.exp(cend[ci * c:ci * c + 1, q * qw:(q + 1) * qw])
                    for ci in range(nck) for q in range(nq)])

    amat = _bmm1(jnp.concatenate([at, rt], axis=1), jnp.concatenate([bdr(bt), bdr(kt)], axis=1), "pil,pjl->pij")
    trow = lax.broadcasted_iota(jnp.int32, (1, c, qw), 1)
    scol = lax.broadcasted_iota(jnp.int32, (1, c, qw), 2) % c
    strict = trow > scol
    incl = trow >= scol
    a_ab = jnp.where(strict, amat[:, :c, :qw], 0.0)
    a_ak = jnp.where(strict, amat[:, :c, qw:], 0.0)
    a_rb = jnp.where(incl, amat[:, c:, :qw], 0.0)
    a_rk = jnp.where(incl, amat[:, c:, qw:], 0.0)

    tinv = jnp.where(trow == scol, 1.0, 0.0) + a_ab
    apow = _bmm1(a_ab, bdr(a_ab.astype(BF16)), nn)
    levels = int(math.log2(c))
    for lvl in range(1, levels):
        pbd = bdr(apow.astype(BF16))
        if lvl < levels - 1:
            sq = _bmm1(jnp.concatenate([apow, tinv], axis=1), pbd, nn)
            apow, tinv = sq[:, :c], tinv + sq[:, c:]
        else:
            tinv = tinv + _bmm1(tinv, pbd, nn)

    v_bd = bdr(vt)
    akv = _bmm1(a_ak, v_bd, nn)
    wu = _bmm1(tinv, jnp.concatenate([bdr(at), bdr(akv.astype(BF16))], axis=2), nn)
    w_a = wu[:, :, :qw]
    u_v = wu[:, :, qw:]

    ht = h_ref[...]
    own = (lax.broadcasted_iota(jnp.int32, (1, qw, qw), 1) // RWKV_HEAD
           == lax.broadcasted_iota(jnp.int32, (1, qw, qw), 2) // RWKV_HEAD)
    o_ck = []
    for ci in range(nck):
        sl = slice(ci * nq, (ci + 1) * nq)
        s1 = _bmm1(jnp.concatenate([w_a[sl], rt[sl]], axis=1), ht, "pil,pjl->pij")
        u = (s1[:, :c] + u_v[sl]).astype(BF16)
        o_ck.append(s1[:, c:] + _bmm1(jnp.concatenate([a_rb[sl], a_rk[sl]], axis=2),
                                      jnp.concatenate([bdr(u), v_bd[sl]], axis=1), nn))
        upd = _bmm1(jnp.concatenate([u, vt[sl]], axis=1), jnp.concatenate([bht[sl], kht[sl]], axis=1),
                    "psv,psk->pvk")
        ht = ht * pc[sl] + jnp.where(own, upd, 0.0)
    h_ref[...] = ht

    rk = rk_ref[...]
    lnw = lnw_ref[...]
    lnb = lnb_ref[...]
    inv_n = 1.0 / RWKV_HEAD
    for t in range(nt):
        sl = slice(t * LANES, (t + 1) * LANES)
        qi, off = divmod(t * LANES, qw)
        ot = jnp.concatenate([o_ck[ci][qi][:, off:off + LANES] for ci in range(nck)], axis=0)
        s0, s1 = _seg2(ot)
        cen = ot - jnp.where(lo2, s0, s1) * inv_n
        q0, q1 = _seg2(cen * cen)
        on = cen * lax.rsqrt(jnp.where(lo2, q0, q1) * inv_n + GN_EPS) * lnw[:, sl] + lnb[:, sl]
        b0, b1 = _seg2(r[:, sl] * k[:, sl] * rk[:, sl])
        on = on + jnp.where(lo2, b0, b1) * v[:, sl]
        o_ref[:, sl] = (on * g[:, sl]).astype(o_ref.dtype)


def _rwkv(x, gmix, w, mu, w0, ww, a0, wa, wg, kkp, kap, rk, lnw, lnb, batch, seq, tb):
    n, d = x.shape
    spb = seq // tb
    return pl.pallas_call(
        _rwkv_kernel,
        grid=(batch, spb),
        in_specs=[pl.BlockSpec((tb, d), lambda bi, si: (bi * spb + si, 0))]
        + [_resident(a) for a in (gmix, w, mu, w0, ww, a0, wa, wg, kkp, kap, rk, lnw, lnb)],
        out_specs=pl.BlockSpec((tb, RWKV_DIM), lambda bi, si: (bi * spb + si, 0)),
        out_shape=jax.ShapeDtypeStruct((n, RWKV_DIM), BF16),
        scratch_shapes=[pltpu.VMEM((RWKV_DIM // MXU_DIM, MXU_DIM, MXU_DIM), F32), pltpu.VMEM((1, w.shape[1]), F32)],
        compiler_params=_cparams(("parallel", "arbitrary")),
        name="rwkv",
    )(x, gmix, w, mu, w0, ww, a0, wa, wg, kkp, kap, rk, lnw, lnb)


def _out_proj_kernel(x_ref, oa_ref, ob_ref, w_ref, o_ref, w16_ref):
    @pl.when(pl.program_id(0) == 0)
    def _():
        w16_ref[...] = w_ref[...].astype(BF16)

    ka = oa_ref.shape[1]
    o_ref[...] = x_ref[...] + _dot(oa_ref[...], w16_ref[:ka, :]) + _dot(ob_ref[...], w16_ref[ka:, :])


def _out_proj(x, oa, ob, w, tm):
    n, d = x.shape
    row_blk = lambda a: pl.BlockSpec((tm, a.shape[1]), lambda i: (i, 0))
    return pl.pallas_call(
        _out_proj_kernel,
        grid=(n // tm,),
        in_specs=[row_blk(x), row_blk(oa), row_blk(ob), _resident(w)],
        out_specs=row_blk(x),
        out_shape=jax.ShapeDtypeStruct((n, d), F32),
        scratch_shapes=[pltpu.VMEM(w.shape, BF16)],
        compiler_params=_cparams(("arbitrary",)),
        name="out_proj",
    )(x, oa, ob, w)


def _ple_kernel(x_ref, g_ref, p_ref, wg_ref, wp_ref, o_ref, wg16_ref, wp16_ref):
    @pl.when(pl.program_id(0) == 0)
    def _():
        wg16_ref[...] = wg_ref[...].astype(BF16)
        wp16_ref[...] = wp_ref[...].astype(BF16)

    x = x_ref[...]
    gate = jax.nn.sigmoid(_dot(_rms(x, g_ref[...]).astype(BF16), wg16_ref[...]))
    o_ref[...] = x + gate * _dot(p_ref[...].astype(BF16), wp16_ref[...])


def _ple(x, g, p, wg, wp, tm):
    n, d = x.shape
    row_blk = lambda a: pl.BlockSpec((tm, a.shape[1]), lambda i: (i, 0))
    return pl.pallas_call(
        _ple_kernel,
        grid=(n // tm,),
        in_specs=[row_blk(x), _resident(g), row_blk(p), _resident(wg), _resident(wp)],
        out_specs=row_blk(x),
        out_shape=jax.ShapeDtypeStruct((n, d), F32),
        scratch_shapes=[pltpu.VMEM(wg.shape, BF16), pltpu.VMEM(wp.shape, BF16)],
        compiler_params=_cparams(("arbitrary",)),
        name="ple",
    )(x, g, p, wg, wp)


def _tile(n, pref):
    t = min(n, pref)
    assert n % t == 0, (n, t)
    return t


def _layer(x, p, norm_ffn1, w1_gate, w1_up, w1_down, norm_mix, w_in, q_a_norm, w_q_b, kv_a_norm, w_kv_b,
           q_norm, k_norm, mu_shift, w0, w_w2, a0, w_a2, w_g2, k_k, k_a, r_k, ln_x_w, ln_x_b, w_out,
           norm_ffn2, w2_gate, w2_up, w2_down, norm_ple, w_ple_gate, w_ple_proj):
    batch, seq, d = x.shape
    n = batch * seq
    row = lambda a: a.reshape(1, -1)
    bf = lambda a: a.astype(BF16)
    mla_in = Q_LORA + KV_LORA + QK_ROPE
    dff = w1_gate.shape[1]
    tf = _tile(dff, 256)
    tm_ffn = _tile(n, 1024)

    xf = x.reshape(n, d)
    x1 = _ffn(xf, row(norm_ffn1), w1_gate, w1_up, w1_down, tm_ffn, tf)

    w_mla = bf(jnp.concatenate([w_in[:, :mla_in], w_in[:, mla_in - QK_ROPE:mla_in]], axis=1))
    wq = w_q_b.reshape(Q_LORA, MLA_HEADS, QK_HEAD)
    wqb = bf(jnp.concatenate([wq[:, :, :QK_NOPE].reshape(Q_LORA, -1), wq[:, :, QK_NOPE:].reshape(Q_LORA, -1)], axis=1))
    inv_freq = 1.0 / (ROPE_BASE ** (jnp.arange(0, QK_ROPE, 2, dtype=F32) / QK_ROPE))
    ang = jnp.arange(seq, dtype=F32)[:, None] * inv_freq[None, :]
    cos, sin = jnp.cos(ang), jnp.sin(ang)
    cos_t = jnp.concatenate([cos, cos, cos, cos], axis=1)
    sin_t = jnp.concatenate([-sin, sin, -sin, sin], axis=1)
    pair = lambda a: row(jnp.concatenate([a, a]))
    q, k, v = _mla_prep(x1, row(norm_mix), w_mla, row(q_a_norm), wqb, row(kv_a_norm), bf(w_kv_b),
                        row(q_norm[:QK_NOPE]), pair(q_norm[QK_NOPE:]), row(k_norm[:QK_NOPE]),
                        pair(k_norm[QK_NOPE:]), cos_t, sin_t, batch, seq, _tile(seq, 512))
    o_mla = _flash(q, k, v, _tile(seq, 2048), _tile(seq, 512)).reshape(n, MLA_HEADS * V_HEAD)

    zero = jnp.zeros_like(w_w2)
    ww = bf(jnp.concatenate([w_w2, zero], axis=0))
    wa = bf(jnp.concatenate([zero, w_a2], axis=0))
    o_rwkv = _rwkv(x1, row(norm_mix), bf(w_in[:, mla_in:]), row(mu_shift), row(w0), ww, row(a0), wa, bf(w_g2),
                   row(k_k), row(k_a), row(r_k), row(ln_x_w), row(ln_x_b), batch, seq, _tile(seq, 4 * CHUNK))

    x2 = _out_proj(x1, o_mla, o_rwkv, w_out, _tile(n, 512))
    x3 = _ffn(x2, row(norm_ffn2), w2_gate, w2_up, w2_down, tm_ffn, tf)
    out = _ple(x3, row(norm_ple), p.reshape(n, -1), w_ple_gate, w_ple_proj, _tile(n, 256))
    return out.reshape(batch, seq, d)


def kernel(x, p, norm_ffn1, w1_gate, w1_up, w1_down, norm_mix, w_in, q_a_norm, w_q_b, kv_a_norm, w_kv_b, q_norm, k_norm, mu_shift, w0, w_w2, a0, w_a2, w_g2, k_k, k_a, r_k, ln_x_w, ln_x_b, w_out, norm_ffn2, w2_gate, w2_up, w2_down, norm_ple, w_ple_gate, w_ple_proj):
    depth = p.shape[0]
    for i in range(depth):
        x = _layer(x, p[i], norm_ffn1[i], w1_gate[i], w1_up[i], w1_down[i], norm_mix[i], w_in[i], q_a_norm[i],
                   w_q_b[i], kv_a_norm[i], w_kv_b[i], q_norm[i], k_norm[i], mu_shift[i], w0[i], w_w2[i], a0[i],
                   w_a2[i], w_g2[i], k_k[i], k_a[i], r_k[i], ln_x_w[i], ln_x_b[i], w_out[i], norm_ffn2[i],
                   w2_gate[i], w2_up[i], w2_down[i], norm_ple[i], w_ple_gate[i], w_ple_proj[i])
    return x
```
